```python
import jax
import jax.numpy as jnp
from jax import lax
import numpy as np

D_MODEL = 4096
BATCH = 16
SEQ = 256
DEPTH = 4
DEC_BATCH = 2
DEC_SEQ = 2048
PAST_LEN = 512

GRID_W = 64
N_MIXERS = 3
N_GLA_LAYERS = (DEPTH + 2) // 3
N_RET_LAYERS = (DEPTH + 1) // 3
N_SWA_LAYERS = DEPTH // 3
CHUNK = 64
ROPE_BASE = 10000.0
NORM_EPS = 1e-6
N_MOD = 6
H_GLA = 8
DK_GLA = 256
DV_GLA = 512
GLA_QK = H_GLA * DK_GLA
GLA_V = H_GLA * DV_GLA
GLA_IN = 2 * GLA_QK + 2 * GLA_V
GLA_RANK = 16
GLA_GATE_NORM = 16.0
H_RET = 16
DK_RET = 256
DV_RET = 512
RET_QK = H_RET * DK_RET
RET_V = H_RET * DV_RET
RET_IN = 2 * RET_QK + 2 * RET_V
N_HEADS_SWA = 64
KV_HEADS = 8
GROUP = N_HEADS_SWA // KV_HEADS
HD_SWA = 64
SWA_Q = N_HEADS_SWA * HD_SWA
SWA_KV = KV_HEADS * HD_SWA
SWA_QKV = SWA_Q + 2 * SWA_KV
WINDOW = 128
Q_BLOCK = 128
NEG_INF = -1e30
N_EXPERTS = 32
TOP_K = 4
D_FF = 1536
SWIGLU_ALPHA = 1.702
SWIGLU_LIMIT = 7.0
MOE_BLOCK = 128

kernel_name = 'hybrid_gla_retnet_swa_moe_diffusion_step'


def rms_norm(x, w=None):
    xf = x.astype(jnp.float32)
    y = xf * lax.rsqrt(jnp.mean(xf * xf, axis=-1, keepdims=True) + NORM_EPS)
    if w is not None:
        y = y * w.astype(jnp.float32)
    return y.astype(x.dtype)


def adaln(cond, w, b):
    m = jax.nn.silu(cond) @ w + b
    return [t[:, None, :] for t in jnp.split(m, N_MOD, axis=-1)]


def modulate(x, shift, scale):
    return x * (1.0 + scale) + shift


def axial_rope(x):
    B, L, H, d = x.shape
    rows = L // GRID_W
    row = jnp.broadcast_to(jnp.arange(rows)[:, None], (rows, GRID_W)).reshape(L)
    col = jnp.broadcast_to(jnp.arange(GRID_W)[None, :], (rows, GRID_W)).reshape(L)
    half = d // 2
    nf = half // 2
    inv = ROPE_BASE ** (-jnp.arange(nf, dtype=jnp.float32) / nf)

    def rot(xa, pos):
        ang = pos.astype(jnp.float32)[:, None] * inv[None, :]
        cos = jnp.cos(ang)[None, :, None, :]
        sin = jnp.sin(ang)[None, :, None, :]
        x1 = xa[..., :nf].astype(jnp.float32)
        x2 = xa[..., nf:].astype(jnp.float32)
        return jnp.concatenate([x1 * cos - x2 * sin, x1 * sin + x2 * cos], axis=-1)

    return jnp.concatenate([rot(x[..., :half], row), rot(x[..., half:], col)], axis=-1).astype(x.dtype)


def to_heads(x, n_heads):
    B, L, _ = x.shape
    return x.reshape(B, L, n_heads, -1).transpose(0, 2, 1, 3)


def chunk_linear_scan(q, k, v, log_a, s0):
    B, H, L, DK = q.shape
    DV = v.shape[-1]
    n = L // CHUNK
    la = jnp.broadcast_to(log_a.astype(jnp.float32), (B, H, L, log_a.shape[-1]))
    qc = q.astype(jnp.float32).reshape(B, H, n, CHUNK, DK)
    kc = k.astype(jnp.float32).reshape(B, H, n, CHUNK, DK)
    vc = v.astype(jnp.float32).reshape(B, H, n, CHUNK, DV)
    b = jnp.cumsum(la.reshape(B, H, n, CHUNK, -1), axis=3)
    b_end = b[:, :, :, -1:, :]
    q_dec = qc * jnp.exp(b)
    k_inv = kc * jnp.exp(-b)
    k_end = kc * jnp.exp(b_end - b)
    causal = jnp.tril(jnp.ones((CHUNK, CHUNK), dtype=bool))
    scores = jnp.where(causal, jnp.einsum('bhncd,bhnsd->bhncs', q_dec, k_inv), 0.0)
    o_intra = jnp.einsum('bhncs,bhnsv->bhncv', scores, vc)
    kv = jnp.einsum('bhncd,bhncv->bhndv', k_end, vc)
    decay = jnp.exp(b_end[:, :, :, 0, :])[..., None]

    def step(s, inp):
        qd, kv_n, dec_n = inp
        o = jnp.einsum('bhcd,bhdv->bhcv', qd, s)
        return dec_n * s + kv_n, o

    s_fin, o_inter = lax.scan(step, s0.astype(jnp.float32),
                              (jnp.moveaxis(q_dec, 2, 0), jnp.moveaxis(kv, 2, 0), jnp.moveaxis(decay, 2, 0)))
    o = o_intra + jnp.moveaxis(o_inter, 0, 2)
    return o.reshape(B, H, L, DV), s_fin


def bidir_scan(q, k, v, la_f, la_b, s0_f, s0_b):
    o_f, s_f = chunk_linear_scan(q, k, v, la_f, s0_f)
    flip = lambda t: jnp.flip(t, axis=2)
    o_b, s_b = chunk_linear_scan(flip(q), flip(k), flip(v), flip(la_b), s0_b)
    return o_f + flip(o_b), s_f, s_b


def gla_mixer(h, s0, w_in, w_gk1, w_gk2, b_gk, norm_w, w_out):
    B, L, _ = h.shape
    q, k, v, g = jnp.split(h @ w_in, [GLA_QK, 2 * GLA_QK, 2 * GLA_QK + GLA_V], axis=-1)
    q = to_heads(q, H_GLA) * (DK_GLA ** -0.5)
    k = to_heads(k, H_GLA)
    v = to_heads(v, H_GLA)

    def log_gate(d):
        z = (h @ w_gk1[d]) @ w_gk2[d] + b_gk[d]
        return to_heads(jax.nn.log_sigmoid(z.astype(jnp.float32)) / GLA_GATE_NORM, H_GLA)

    o, s_f, s_b = bidir_scan(q, k, v, log_gate(0), log_gate(1), s0[:, 0], s0[:, 1])
    o = rms_norm(o, norm_w).transpose(0, 2, 1, 3).reshape(B, L, GLA_V)
    out = (o * jax.nn.silu(g.astype(jnp.float32))) @ w_out
    return out.astype(h.dtype), jnp.stack([s_f, s_b], axis=1)


def ret_mixer(h, s0, rotary, w_in, decay_logit, w_out):
    B, L, _ = h.shape
    q, k, v, g = jnp.split(h @ w_in, [RET_QK, 2 * RET_QK, 2 * RET_QK + RET_V], axis=-1)
    q = q.reshape(B, L, H_RET, DK_RET)
    k = k.reshape(B, L, H_RET, DK_RET)
    if rotary:
        q, k = axial_rope(q), axial_rope(k)
    q = q.transpose(0, 2, 1, 3)
    k = k.transpose(0, 2, 1, 3) * (DK_RET ** -0.5)
    v = to_heads(v, H_RET)
    log_gamma = jax.nn.log_sigmoid(decay_logit.astype(jnp.float32))
    la_f = log_gamma[0][None, :, None, None]
    la_b = log_gamma[1][None, :, None, None]
    o, s_f, s_b = bidir_scan(q, k, v, la_f, la_b, s0[:, 0], s0[:, 1])
    o = rms_norm(o).transpose(0, 2, 1, 3).reshape(B, L, RET_V)
    out = (jax.nn.silu(g.astype(jnp.float32)) * o) @ w_out
    return out.astype(h.dtype), jnp.stack([s_f, s_b], axis=1)


def swa_project(h, w_qkv, b_qkv, rotary):
    B, L, _ = h.shape
    q, k, v = jnp.split(h @ w_qkv + b_qkv, [SWA_Q, SWA_Q + SWA_KV], axis=-1)
    q = q.reshape(B, L, N_HEADS_SWA, HD_SWA)
    k = k.reshape(B, L, KV_HEADS, HD_SWA)
    v = v.reshape(B, L, KV_HEADS, HD_SWA)
    if rotary:
        q, k = axial_rope(q), axial_rope(k)
    return q.reshape(B, L, KV_HEADS, GROUP, HD_SWA), k, v


def sink_softmax_attend(q, segments, sinks):
    scores = []
    for k, v, mask in segments:
        s = jnp.einsum('bqkgd,bskd->bkgqs', q, k).astype(jnp.float32) * (HD_SWA ** -0.5)
        if mask is not None:
            s = jnp.where(mask, s, NEG_INF)
        scores.append(s)
    B, KVH, G, Q = scores[0].shape[:4]
    sink = jnp.broadcast_to(sinks.astype(jnp.float32).reshape(1, KVH, G, 1, 1), (B, KVH, G, Q, 1))
    p = jax.nn.softmax(jnp.concatenate(scores + [sink], axis=-1), axis=-1)
    out = None
    off = 0
    for (k, v, _), s in zip(segments, scores):
        n = s.shape[-1]
        term = jnp.einsum('bkgqs,bskd->bqkgd', p[..., off:off + n].astype(v.dtype), v)
        out = term if out is None else out + term
        off += n
    return out


def ctx_attention(q, k, v, sinks):
    B, L = q.shape[:2]
    nq = L // Q_BLOCK
    qb = jnp.moveaxis(q.reshape(B, nq, Q_BLOCK, KV_HEADS, GROUP, HD_SWA), 1, 0)
    o = lax.map(lambda qi: sink_softmax_attend(qi, [(k, v, None)], sinks), qb)
    return jnp.moveaxis(o, 0, 1).reshape(B, L, SWA_Q)


def latent_attention(q, k, v, k_ctx, v_ctx, sinks):
    B, L = q.shape[:2]
    nq = L // Q_BLOCK
    qb = jnp.moveaxis(q.reshape(B, nq, Q_BLOCK, KV_HEADS, GROUP, HD_SWA), 1, 0)
    pad = ((0, 0), (Q_BLOCK, Q_BLOCK), (0, 0), (0, 0))
    kp = jnp.pad(k, pad)
    vp = jnp.pad(v, pad)
    offs = jnp.arange(3 * Q_BLOCK) - Q_BLOCK
    qoff = jnp.arange(Q_BLOCK)

    def blk(args):
        qi, j = args
        start = j * Q_BLOCK
        kw = lax.dynamic_slice_in_dim(kp, start, 3 * Q_BLOCK, axis=1)
        vw = lax.dynamic_slice_in_dim(vp, start, 3 * Q_BLOCK, axis=1)
        kpos = start + offs
        qpos = start + qoff
        mask = ((jnp.abs(qpos[:, None] - kpos[None, :]) <= WINDOW)
                & (kpos >= 0)[None, :] & (kpos < L)[None, :])
        return sink_softmax_attend(qi, [(kw, vw, mask), (k_ctx, v_ctx, None)], sinks)

    o = lax.map(blk, (qb, jnp.arange(nq)))
    return jnp.moveaxis(o, 0, 1).reshape(B, L, SWA_Q)


def swa_context(h, w_qkv, b_qkv, sinks, w_o, b_o):
    q, k, v = swa_project(h, w_qkv, b_qkv, rotary=False)
    o = ctx_attention(q, k, v, sinks)
    return (o @ w_o + b_o).astype(h.dtype), k, v


def swa_latent(h, k_ctx, v_ctx, w_qkv, b_qkv, sinks, w_o, b_o):
    q, k, v = swa_project(h, w_qkv, b_qkv, rotary=True)
    o = latent_attention(q, k, v, k_ctx.astype(k.dtype), v_ctx.astype(v.dtype), sinks)
    return (o @ w_o + b_o).astype(h.dtype)


def moe_ffn(h, w_r, b_r, w_gu, b_gu, w_dn, b_dn):
    B, L, D = h.shape
    T = B * L
    xt = h.reshape(T, D)
    logits = (xt @ w_r + b_r).astype(jnp.float32)
    top_v, top_i = lax.top_k(logits, TOP_K)
    gates = jax.nn.softmax(top_v, axis=-1)
    n_asg = T * TOP_K
    flat_e = top_i.reshape(n_asg).astype(jnp.int32)
    flat_tok = jnp.arange(n_asg, dtype=jnp.int32) // TOP_K
    flat_g = gates.reshape(n_asg)
    order = jnp.argsort(flat_e)
    se, stok, sg = flat_e[order], flat_tok[order], flat_g[order]
    counts = jnp.bincount(flat_e, length=N_EXPERTS).astype(jnp.int32)
    starts = jnp.cumsum(counts) - counts
    pcounts = (counts + MOE_BLOCK - 1) // MOE_BLOCK * MOE_BLOCK
    pends = jnp.cumsum(pcounts)
    pstarts = pends - pcounts
    dest = pstarts[se] + jnp.arange(n_asg, dtype=jnp.int32) - starts[se]
    nb = -(-(n_asg + N_EXPERTS * (MOE_BLOCK - 1)) // MOE_BLOCK)
    P = nb * MOE_BLOCK
    slot_tok = jnp.full((P,), T, dtype=jnp.int32).at[dest].set(stok)
    slot_g = jnp.zeros((P,), jnp.float32).at[dest].set(sg)
    blk_e = jnp.minimum(jnp.searchsorted(pends, jnp.arange(nb, dtype=jnp.int32) * MOE_BLOCK, side='right'),
                        N_EXPERTS - 1)
    xpad = jnp.concatenate([xt, jnp.zeros((1, D), xt.dtype)], axis=0)
    xb = xpad[slot_tok].reshape(nb, MOE_BLOCK, D)

    def expert_block(args):
        xblk, e = args
        gu = xblk @ w_gu[e] + b_gu[e]
        x_glu = jnp.minimum(gu[:, :D_FF], SWIGLU_LIMIT)
        x_lin = jnp.clip(gu[:, D_FF:], -SWIGLU_LIMIT, SWIGLU_LIMIT)
        hidden = x_glu * jax.nn.sigmoid(SWIGLU_ALPHA * x_glu) * (x_lin + 1.0)
        return hidden @ w_dn[e] + b_dn[e]

    yb = lax.map(expert_block, (xb, blk_e)).reshape(P, D)
    y = jax.ops.segment_sum(yb * slot_g[:, None].astype(yb.dtype), slot_tok, num_segments=T + 1)[:T]
    return y.reshape(B, L, D).astype(h.dtype)


def setup_inputs(seed: int = 0) -> dict:
    key = jax.random.key(seed)
    ks = iter(jax.random.split(key, 40))
    f32 = jnp.float32

    def nrm(shape, scale):
        return jax.random.normal(next(ks), shape, f32) * scale

    e = 5.0 + jnp.arange(H_RET, dtype=f32)
    base = jnp.log(2.0 ** e - 1.0)
    ret_base = jnp.stack([base, base[::-1]], axis=0)
    return {
        'x_prompt': nrm((BATCH, SEQ, D_MODEL), 1.0),
        'x_sample': nrm((DEC_BATCH, DEC_SEQ, D_MODEL), 1.0),
        'state_gla': nrm((DEC_BATCH, N_GLA_LAYERS, 2, H_GLA, DK_GLA, DV_GLA), 0.5),
        'state_ret': nrm((DEC_BATCH, N_RET_LAYERS, 2, H_RET, DK_RET, DV_RET), 0.5),
        'cache_k': nrm((DEC_BATCH, N_SWA_LAYERS, PAST_LEN, KV_HEADS, HD_SWA), 1.0),
        'cache_v': nrm((DEC_BATCH, N_SWA_LAYERS, PAST_LEN, KV_HEADS, HD_SWA), 1.0),
        'c': nrm((DEC_BATCH, D_MODEL), 1.0),
        'c_ctx': nrm((D_MODEL,), 1.0),
        'norm_mix_w': 1.0 + nrm((DEPTH, D_MODEL), 0.02),
        'norm_ffn_w': 1.0 + nrm((DEPTH, D_MODEL), 0.02),
        'w_ada': nrm((DEPTH, D_MODEL, N_MOD * D_MODEL), 0.5 * D_MODEL ** -0.5),
        'b_ada': nrm((DEPTH, N_MOD * D_MODEL), 0.01),
        'gla_w_in': nrm((N_GLA_LAYERS, D_MODEL, GLA_IN), D_MODEL ** -0.5),
        'gla_w_gk1': nrm((N_GLA_LAYERS, 2, D_MODEL, GLA_RANK), D_MODEL ** -0.5),
        'gla_w_gk2': nrm((N_GLA_LAYERS, 2, GLA_RANK, GLA_QK), GLA_RANK ** -0.5),
        'gla_b_gk': nrm((N_GLA_LAYERS, 2, GLA_QK), 0.1),
        'gla_norm_w': 1.0 + nrm((N_GLA_LAYERS, DV_GLA), 0.02),
        'gla_w_out': nrm((N_GLA_LAYERS, GLA_V, D_MODEL), GLA_V ** -0.5),
        'ret_w_in': nrm((N_RET_LAYERS, D_MODEL, RET_IN), D_MODEL ** -0.5),
        'ret_decay_logit': ret_base[None] + nrm((N_RET_LAYERS, 2, H_RET), 0.01),
        'ret_w_out': nrm((N_RET_LAYERS, RET_V, D_MODEL), RET_V ** -0.5),
        'swa_w_qkv': nrm((N_SWA_LAYERS, D_MODEL, SWA_QKV), D_MODEL ** -0.5),
        'swa_b_qkv': nrm((N_SWA_LAYERS, SWA_QKV), 0.01),
        'swa_sinks': nrm((N_SWA_LAYERS, N_HEADS_SWA), 1.0),
        'swa_w_o': nrm((N_SWA_LAYERS, SWA_Q, D_MODEL), SWA_Q ** -0.5),
        'swa_b_o': nrm((N_SWA_LAYERS, D_MODEL), 0.01),
        'moe_w_router': nrm((DEPTH, D_MODEL, N_EXPERTS), D_MODEL ** -0.5),
        'moe_b_router': nrm((DEPTH, N_EXPERTS), 0.01),
        'moe_w_gate_up': nrm((DEPTH, N_EXPERTS, D_MODEL, 2 * D_FF), D_MODEL ** -0.5),
        'moe_b_gate_up': nrm((DEPTH, N_EXPERTS, 2 * D_FF), 0.01),
        'moe_w_down': nrm((DEPTH, N_EXPERTS, D_FF, D_MODEL), D_FF ** -0.5),
        'moe_b_down': nrm((DEPTH, N_EXPERTS, D_MODEL), 0.01),
        'final_norm_w': 1.0 + nrm((D_MODEL,), 0.02),
    }


def reference(x_prompt, x_sample, state_gla, state_ret, cache_k, cache_v, c, c_ctx,
              norm_mix_w, norm_ffn_w, w_ada, b_ada,
              gla_w_in, gla_w_gk1, gla_w_gk2, gla_b_gk, gla_norm_w, gla_w_out,
              ret_w_in, ret_decay_logit, ret_w_out,
              swa_w_qkv, swa_b_qkv, swa_sinks, swa_w_o, swa_b_o,
              moe_w_router, moe_b_router, moe_w_gate_up, moe_b_gate_up, moe_w_down, moe_b_down,
              final_norm_w):
    xp, xs = x_prompt, x_sample
    n_p = xp.shape[0]
    new_gla, new_ret, new_k, new_v = [], [], [], []
    for layer in range(DEPTH):
        kind, idx = layer % N_MIXERS, layer // N_MIXERS
        mp = adaln(c_ctx[None, :], w_ada[layer], b_ada[layer])
        ms = adaln(c, w_ada[layer], b_ada[layer])
        hp = modulate(rms_norm(xp, norm_mix_w[layer]), mp[0], mp[1])
        hs = modulate(rms_norm(xs, norm_mix_w[layer]), ms[0], ms[1])
        if kind == 0:
            gp = (gla_w_in[idx], gla_w_gk1[idx], gla_w_gk2[idx], gla_b_gk[idx], gla_norm_w[idx], gla_w_out[idx])
            s0 = jnp.zeros((n_p, 2, H_GLA, DK_GLA, DV_GLA), jnp.float32)
            op, st = gla_mixer(hp, s0, *gp)
            os_, _ = gla_mixer(hs, state_gla[:, idx], *gp)
            new_gla.append(st)
        elif kind == 1:
            s0 = jnp.zeros((n_p, 2, H_RET, DK_RET, DV_RET), jnp.float32)
            op, st = ret_mixer(hp, s0, False, ret_w_in[idx], ret_decay_logit[idx], ret_w_out[idx])
            os_, _ = ret_mixer(hs, state_ret[:, idx], True, ret_w_in[idx], ret_decay_logit[idx], ret_w_out[idx])
            new_ret.append(st)
        else:
            sp = (swa_w_qkv[idx], swa_b_qkv[idx], swa_sinks[idx].reshape(KV_HEADS, GROUP), swa_w_o[idx], swa_b_o[idx])
            op, kc, vc = swa_context(hp, *sp)
            os_ = swa_latent(hs, cache_k[:, idx], cache_v[:, idx], *sp)
            new_k.append(kc)
            new_v.append(vc)
        xp = xp + mp[2] * op
        xs = xs + ms[2] * os_
        hp = modulate(rms_norm(xp, norm_ffn_w[layer]), mp[3], mp[4])
        hs = modulate(rms_norm(xs, norm_ffn_w[layer]), ms[3], ms[4])
        mo = (moe_w_router[layer], moe_b_router[layer], moe_w_gate_up[layer], moe_b_gate_up[layer],
              moe_w_down[layer], moe_b_down[layer])
        xp = xp + mp[5] * moe_ffn(hp, *mo)
        xs = xs + ms[5] * moe_ffn(hs, *mo)
    y_prompt = rms_norm(xp, final_norm_w)
    y_sample = rms_norm(xs, final_norm_w)
    new_state_gla = jnp.stack(new_gla, axis=1)
    new_state_ret = jnp.stack(new_ret, axis=1)
    new_cache_k = jnp.stack(new_k, axis=1)
    new_cache_v = jnp.stack(new_v, axis=1)
    return (y_prompt, y_sample, new_state_gla, new_state_ret, new_cache_k, new_cache_v)
```

```python
import functools
from typing import NamedTuple

import numpy as np
import jax
import jax.numpy as jnp
from jax import lax
from jax.experimental import pallas as pl
from jax.experimental.pallas import tpu as pltpu

F32 = jnp.float32
BF16 = jnp.bfloat16

D_MODEL = 4096
DEPTH = 4
GRID_W = 64
N_MIXERS = 3
CHUNK = 64
ROPE_BASE = 10000.0
NORM_EPS = 1e-6
N_MOD = 6
H_GLA, DK_GLA, DV_GLA = 8, 256, 512
GLA_QK = H_GLA * DK_GLA
GLA_V = H_GLA * DV_GLA
GLA_RANK = 16
GLA_GATE_NORM = 16.0
H_RET, DK_RET, DV_RET = 16, 256, 512
RET_QK = H_RET * DK_RET
RET_V = H_RET * DV_RET
N_HEADS_SWA, KV_HEADS, HD_SWA = 64, 8, 64
GROUP = N_HEADS_SWA // KV_HEADS
SWA_Q = N_HEADS_SWA * HD_SWA
SWA_KV = KV_HEADS * HD_SWA
WINDOW = 128
Q_BLOCK = 128
NEG_INF = -1e30
N_EXPERTS = 32
TOP_K = 4
D_FF = 1536
SWIGLU_ALPHA = 1.702
SWIGLU_LIMIT = 7.0

LANES = 128
VMEM_LIMIT = 56 * 1024 * 1024
COND_ROWS = 16
MOE_BM = 256
RANK_PAD = 128


class Layout(NamedTuple):
    n_p: int
    l_p: int
    n_s: int
    l_s: int

    @property
    def t_p(self):
        return self.n_p * self.l_p

    @property
    def t(self):
        return self.n_p * self.l_p + self.n_s * self.l_s


def _cparams(sem):
    return pltpu.CompilerParams(dimension_semantics=sem, vmem_limit_bytes=VMEM_LIMIT)


def _cond_of_block(lay, bm):
    assert lay.t_p % bm == 0 and lay.l_s % bm == 0
    npb = lay.t_p // bm
    per = lay.l_s // bm
    return lambda i: jnp.where(i < npb, 0, 1 + (i - npb) // per)


def _pick(n, prefs):
    for p in prefs:
        if n % p == 0:
            return p
    return n


def _adaln_kernel(c_ref, w_ref, b_ref, o_ref):
    c = c_ref[...]
    s = (c * jax.nn.sigmoid(c)).astype(BF16)
    o_ref[...] = jnp.dot(s, w_ref[...].astype(BF16), preferred_element_type=F32) + b_ref[...]


def adaln_all(cond, w_ada, b_ada):
    depth, d, n = w_ada.shape
    tn = 512
    return pl.pallas_call(
        _adaln_kernel,
        out_shape=jax.ShapeDtypeStruct((depth, COND_ROWS, n), F32),
        grid=(depth, n // tn),
        in_specs=[
            pl.BlockSpec((COND_ROWS, d), lambda l, j: (0, 0)),
            pl.BlockSpec((None, d, tn), lambda l, j: (l, 0, j)),
            pl.BlockSpec((None, 1, tn), lambda l, j: (l, 0, j)),
        ],
        out_specs=pl.BlockSpec((None, COND_ROWS, tn), lambda l, j: (l, 0, j)),
        compiler_params=_cparams(("arbitrary", "arbitrary")),
        name="adaln",
    )(cond, w_ada, b_ada.reshape(depth, 1, n))


def _normmod_kernel(x_ref, nw_ref, sh_ref, sc_ref, o_ref):
    x = x_ref[...]
    y = x * lax.rsqrt(jnp.mean(x * x, axis=-1, keepdims=True) + NORM_EPS) * nw_ref[...]
    o_ref[...] = (y * (1.0 + sc_ref[...]) + sh_ref[...]).astype(o_ref.dtype)


def norm_modulate(x, norm_w, layer, mods, shift_slot, lay, out_dtype):
    t, d = x.shape
    bm = 256
    cond = _cond_of_block(lay, bm)
    return pl.pallas_call(
        _normmod_kernel,
        out_shape=jax.ShapeDtypeStruct((t, d), out_dtype),
        grid=(t // bm,),
        in_specs=[
            pl.BlockSpec((bm, d), lambda i: (i, 0)),
            pl.BlockSpec((None, 1, d), lambda i: (layer, 0, 0)),
            pl.BlockSpec((None, None, None, 1, d), lambda i: (layer, cond(i), shift_slot, 0, 0)),
            pl.BlockSpec((None, None, None, 1, d), lambda i: (layer, cond(i), shift_slot + 1, 0, 0)),
        ],
        out_specs=pl.BlockSpec((bm, d), lambda i: (i, 0)),
        compiler_params=_cparams(("arbitrary",)),
        name="norm_modulate",
    )(x, norm_w, mods, mods)


def _final_norm_kernel(x_ref, nw_ref, o_ref):
    x = x_ref[...]
    o_ref[...] = x * lax.rsqrt(jnp.mean(x * x, axis=-1, keepdims=True) + NORM_EPS) * nw_ref[...]


def final_norm(x, w):
    t, d = x.shape
    bm = 256
    return pl.pallas_call(
        _final_norm_kernel,
        out_shape=jax.ShapeDtypeStruct((t, d), F32),
        grid=(t // bm,),
        in_specs=[pl.BlockSpec((bm, d), lambda i: (i, 0)), pl.BlockSpec((1, d), lambda i: (0, 0))],
        out_specs=pl.BlockSpec((bm, d), lambda i: (i, 0)),
        compiler_params=_cparams(("arbitrary",)),
        name="final_norm",
    )(x, w.reshape(1, d))


def _mm_kernel(*refs, has_bias, resid):
    it = iter(refs)
    x_ref, w_ref = next(it), next(it)
    b_ref = next(it) if has_bias else None
    res_ref, gate_ref = (next(it), next(it)) if resid else (None, None)
    o_ref, wbf_ref = next(it), next(it)

    @pl.when(pl.program_id(1) == 0)
    def _():
        wbf_ref[...] = w_ref[...].astype(BF16)

    acc = jnp.dot(x_ref[...], wbf_ref[...], preferred_element_type=F32)
    if has_bias:
        acc = acc + b_ref[...]
    if resid:
        acc = res_ref[...] + gate_ref[...] * acc
    o_ref[...] = acc.astype(o_ref.dtype)


def matmul(x, w, idx, *, bias=None, resid=None, out_dtype=F32, lay=None):
    m, k = x.shape
    n = w.shape[-1]
    rows = m if lay is None else np.gcd(lay.t_p, lay.l_s)
    if k * 512 * 4 <= 8 * 1024 * 1024:
        bn, bm = _pick(n, (512, 256, 128)), _pick(rows, (1024, 512, 256, 128))
    else:
        bn, bm = _pick(n, (256, 128)), _pick(rows, (512, 256, 128))
    in_specs = [
        pl.BlockSpec((bm, k), lambda j, i: (i, 0)),
        pl.BlockSpec((None, k, bn), lambda j, i: (idx, 0, j)),
    ]
    args = [x, w]
    if bias is not None:
        in_specs.append(pl.BlockSpec((None, 1, bn), lambda j, i: (idx, 0, j)))
        args.append(bias.reshape(bias.shape[0], 1, n))
    if resid is not None:
        res, mods, layer, slot = resid
        cond = _cond_of_block(lay, bm)
        in_specs.append(pl.BlockSpec((bm, bn), lambda j, i: (i, j)))
        in_specs.append(pl.BlockSpec((None, None, None, 1, bn), lambda j, i: (layer, cond(i), slot, 0, j)))
        args += [res, mods]
    return pl.pallas_call(
        functools.partial(_mm_kernel, has_bias=bias is not None, resid=resid is not None),
        out_shape=jax.ShapeDtypeStruct((m, n), out_dtype),
        grid=(n // bn, m // bm),
        in_specs=in_specs,
        out_specs=pl.BlockSpec((bm, bn), lambda j, i: (i, j)),
        scratch_shapes=[pltpu.VMEM((k, bn), BF16)],
        compiler_params=_cparams(("arbitrary", "arbitrary")),
        name="matmul",
    )(*args)


def _log_sigmoid(z):
    return jnp.minimum(z, 0.0) - jnp.log1p(jnp.exp(-jnp.abs(z)))


def _split3(x):
    hi = x.astype(BF16)
    r1 = x - hi.astype(F32)
    mid = r1.astype(BF16)
    lo = (r1 - mid.astype(F32)).astype(BF16)
    return hi, mid, lo


def _dot_nt(a, b):
    return lax.dot_general(a, b, (((1,), (1,)), ((), ())), preferred_element_type=F32)


def _dot_tn(a, b):
    return lax.dot_general(a, b, (((0,), (0,)), ((), ())), preferred_element_type=F32)


def _swap_halves(x):
    parts = [pltpu.roll(x[:, o:o + LANES], LANES // 2, 1) for o in range(0, x.shape[1], LANES)]
    return parts[0] if len(parts) == 1 else jnp.concatenate(parts, axis=1)


def _scan_kernel(*refs, seq_len, dk, dv, gla, rope, has_s0, want_state):
    it = iter(refs)
    q_ref, k_ref, v_ref, g_ref = next(it), next(it), next(it), next(it)
    if gla:
        r_ref, w2_ref, bgk_ref, nw_ref = next(it), next(it), next(it), next(it)
    else:
        dl_ref = next(it)
    if rope:
        cos_ref, sin_ref = next(it), next(it)
    s0_ref = next(it) if has_s0 else None
    o_ref = next(it)
    st_ref = next(it) if want_state else None
    st_scr, oacc = next(it), next(it)

    n_chunks = seq_len // CHUNK
    head = pl.program_id(1)
    row = lax.broadcasted_iota(jnp.int32, (CHUNK, CHUNK), 0)
    col = lax.broadcasted_iota(jnp.int32, (CHUNK, CHUNK), 1)
    scale = dk ** -0.5

    for d in range(2):
        keep = (col <= row) if d == 0 else (col >= row)
        tri = jnp.where(keep, 1.0, 0.0).astype(BF16)
        if not gla:
            z = jnp.full((CHUNK, dk), dl_ref[d, head], F32)
            la_const = _log_sigmoid(z)
        if has_s0:
            st_scr[...] = s0_ref[d].T
        else:
            st_scr[...] = jnp.zeros((dv, dk), F32)

        def chunk(i, carry, d=d, keep=keep, tri=tri):
            c = i if d == 0 else n_chunks - 1 - i
            rows = pl.ds(pl.multiple_of(c * CHUNK, CHUNK), CHUNK)
            qc, kc = q_ref[rows, :], k_ref[rows, :]
            vc = v_ref[rows, :].astype(BF16)
            if rope:
                cs, sn = cos_ref[rows, :], sin_ref[rows, :]
                qc = qc * cs + _swap_halves(qc) * sn
                kc = kc * cs + _swap_halves(kc) * sn
            if gla:
                qc = qc * scale
                rr = r_ref[rows, d * RANK_PAD:(d + 1) * RANK_PAD].astype(BF16)
                z = jnp.dot(rr, w2_ref[d].astype(BF16), preferred_element_type=F32) + bgk_ref[d]
                la = _log_sigmoid(z) * (1.0 / GLA_GATE_NORM)
            else:
                kc = kc * scale
                la = la_const
            hi, mid, lo = _split3(la)
            b = (jnp.dot(tri, hi, preferred_element_type=F32) + jnp.dot(tri, mid, preferred_element_type=F32)
                 + jnp.dot(tri, lo, preferred_element_type=F32))
            b_end = jnp.sum(la, axis=0, keepdims=True)
            q_dec = (qc * jnp.exp(b)).astype(BF16)
            k_inv = (kc * jnp.exp(-b)).astype(BF16)
            k_end = (kc * jnp.exp(b_end - b)).astype(BF16)
            scores = jnp.where(keep, _dot_nt(q_dec, k_inv), 0.0).astype(BF16)
            st = st_scr[...]
            o = jnp.dot(scores, vc, preferred_element_type=F32) + _dot_nt(q_dec, st.astype(BF16))
            st_scr[...] = st * jnp.exp(b_end) + _dot_tn(vc, k_end)
            if d == 0:
                oacc[rows, :] = o
            else:
                tot = oacc[rows, :] + o
                y = tot * lax.rsqrt(jnp.mean(tot * tot, axis=-1, keepdims=True) + NORM_EPS)
                if gla:
                    y = y * nw_ref[...]
                gg = g_ref[rows, :]
                o_ref[rows, :] = (y * (gg * jax.nn.sigmoid(gg))).astype(o_ref.dtype)
            return carry

        lax.fori_loop(0, n_chunks, chunk, 0)
        if want_state:
            st_ref[d] = st_scr[...].T


def linear_scan(y, *, seq_len, n_seq, row_block0, n_heads, dk, dv, gla, rope_tabs=None, s0=None, s0_idx=0,
                want_state=False, gla_args=None, decay_logit=None):
    qk = n_heads * dk
    kb, vb, gb = qk // dk, 2 * qk // dv, (2 * qk + n_heads * dv) // dv
    in_specs = [
        pl.BlockSpec((seq_len, dk), lambda b, h: (row_block0 + b, h)),
        pl.BlockSpec((seq_len, dk), lambda b, h: (row_block0 + b, kb + h)),
        pl.BlockSpec((seq_len, dv), lambda b, h: (row_block0 + b, vb + h)),
        pl.BlockSpec((seq_len, dv), lambda b, h: (row_block0 + b, gb + h)),
    ]
    args = [y, y, y, y]
    if gla:
        r, w2, bgk, nw = gla_args
        in_specs += [
            pl.BlockSpec((seq_len, 2 * RANK_PAD), lambda b, h: (row_block0 + b, 0)),
            pl.BlockSpec((2, RANK_PAD, dk), lambda b, h: (0, 0, h)),
            pl.BlockSpec((2, 1, dk), lambda b, h: (0, 0, h)),
            pl.BlockSpec((1, dv), lambda b, h: (0, 0)),
        ]
        args += [r, w2, bgk, nw]
    else:
        in_specs.append(pl.BlockSpec(memory_space=pltpu.SMEM))
        args.append(decay_logit)
    if rope_tabs is not None:
        in_specs += [pl.BlockSpec((seq_len, dk), lambda b, h: (0, 0))] * 2
        args += list(rope_tabs)
    if s0 is not None:
        in_specs.append(pl.BlockSpec((None, None, 2, None, dk, dv), lambda b, h: (b, s0_idx, 0, h, 0, 0)))
        args.append(s0)
    out_shape = [jax.ShapeDtypeStruct((n_seq * seq_len, n_heads * dv), BF16)]
    out_specs = [pl.BlockSpec((seq_len, dv), lambda b, h: (b, h))]
    if want_state:
        out_shape.append(jax.ShapeDtypeStruct((n_seq, 2, n_heads, dk, dv), F32))
        out_specs.append(pl.BlockSpec((None, 2, None, dk, dv), lambda b, h: (b, 0, h, 0, 0)))
    outs = pl.pallas_call(
        functools.partial(_scan_kernel, seq_len=seq_len, dk=dk, dv=dv, gla=gla, rope=rope_tabs is not None,
                          has_s0=s0 is not None, want_state=want_state),
        out_shape=out_shape,
        grid=(n_seq, n_heads),
        in_specs=in_specs,
        out_specs=out_specs,
        scratch_shapes=[pltpu.VMEM((dv, dk), F32), pltpu.VMEM((seq_len, dv), F32)],
        compiler_params=_cparams(("arbitrary", "arbitrary")),
        name="linear_scan",
    )(*args)
    return outs[0], (outs[1] if want_state else None)


def _rope_tables(seq_len, d):
    half, nf = d // 2, d // 4
    pos = np.arange(seq_len)
    inv = ROPE_BASE ** (-np.arange(nf, dtype=np.float32) / nf)
    cos_l, sin_l = [], []
    for p in (pos // GRID_W, pos % GRID_W):
        ang = p.astype(np.float32)[:, None] * inv[None, :]
        c, s = np.cos(ang), np.sin(ang)
        cos_l += [c, c]
        sin_l += [-s, s]
    return (jnp.asarray(np.concatenate(cos_l, axis=1), F32), jnp.asarray(np.concatenate(sin_l, axis=1), F32))


def _rope64_kernel(x_ref, cos_ref, s1_ref, s2_ref, o_ref):
    cs, s1, s2 = cos_ref[...], s1_ref[...], s2_ref[...]
    for o in range(0, x_ref.shape[1], LANES):
        x = x_ref[:, o:o + LANES]
        o_ref[:, o:o + LANES] = (x * cs + pltpu.roll(x, LANES - HD_SWA // 4, 1) * s1
                                 + pltpu.roll(x, HD_SWA // 4, 1) * s2)


def rope64(x, seq_len):
    t, w = x.shape
    nf = HD_SWA // 4
    pos = np.arange(seq_len)
    inv = ROPE_BASE ** (-np.arange(nf, dtype=np.float32) / nf)
    ar = (pos // GRID_W).astype(np.float32)[:, None] * inv[None, :]
    ac = (pos % GRID_W).astype(np.float32)[:, None] * inv[None, :]
    zero = np.zeros_like(ar)
    cos = np.concatenate([np.cos(ar), np.cos(ar), np.cos(ac), np.cos(ac)], axis=1)
    s1 = np.concatenate([-np.sin(ar), zero, -np.sin(ac), zero], axis=1)
    s2 = np.concatenate([zero, np.sin(ar), zero, np.sin(ac)], axis=1)
    tabs = [jnp.asarray(np.tile(a, (1, LANES // HD_SWA)), F32) for a in (cos, s1, s2)]
    bm = _pick(seq_len, (256, 128))
    per = seq_len // bm
    return pl.pallas_call(
        _rope64_kernel,
        out_shape=jax.ShapeDtypeStruct((t, w), F32),
        grid=(t // bm,),
        in_specs=[pl.BlockSpec((bm, w), lambda i: (i, 0))] + [pl.BlockSpec((bm, LANES), lambda i: (i % per, 0))] * 3,
        out_specs=pl.BlockSpec((bm, w), lambda i: (i, 0)),
        compiler_params=_cparams(("arbitrary",)),
        name="rope64",
    )(x, *tabs)


def _attn_kernel(*refs, seq_len, q_rows, windowed):
    it = iter(refs)
    q_ref, k_ref, v_ref = next(it), next(it), next(it)
    kc_ref, vc_ref = (next(it), next(it)) if windowed else (None, None)
    sink_ref, o_ref = next(it), next(it)
    rows = GROUP * q_rows
    scale = HD_SWA ** -0.5
    q = q_ref[...].reshape(rows, HD_SWA).astype(BF16)
    sink = sink_ref[...]
    if windowed:
        j = pl.program_id(2)
        span = 3 * Q_BLOCK
        ws = pl.multiple_of(jnp.clip(j * Q_BLOCK - Q_BLOCK, 0, seq_len - span), Q_BLOCK)
        kw = k_ref[pl.ds(ws, span), :].astype(BF16)
        vw = v_ref[pl.ds(ws, span), :].astype(BF16)
        qpos = j * Q_BLOCK + (lax.broadcasted_iota(jnp.int32, (rows, span), 0) & (q_rows - 1))
        kpos = ws + lax.broadcasted_iota(jnp.int32, (rows, span), 1)
        s1 = jnp.where(jnp.abs(qpos - kpos) <= WINDOW, _dot_nt(q, kw) * scale, NEG_INF)
        s2 = _dot_nt(q, kc_ref[...].astype(BF16)) * scale
        m = jnp.maximum(jnp.maximum(jnp.max(s1, axis=-1, keepdims=True), jnp.max(s2, axis=-1, keepdims=True)), sink)
        p1, p2 = jnp.exp(s1 - m), jnp.exp(s2 - m)
        den = jnp.sum(p1, axis=-1, keepdims=True) + jnp.sum(p2, axis=-1, keepdims=True) + jnp.exp(sink - m)
        o = (jnp.dot(p1.astype(BF16), vw, preferred_element_type=F32)
             + jnp.dot(p2.astype(BF16), vc_ref[...].astype(BF16), preferred_element_type=F32))
    else:
        s1 = _dot_nt(q, k_ref[...].astype(BF16)) * scale
        m = jnp.maximum(jnp.max(s1, axis=-1, keepdims=True), sink)
        p1 = jnp.exp(s1 - m)
        den = jnp.sum(p1, axis=-1, keepdims=True) + jnp.exp(sink - m)
        o = jnp.dot(p1.astype(BF16), v_ref[...].astype(BF16), preferred_element_type=F32)
    o_ref[...] = (o / den).reshape(GROUP, q_rows, HD_SWA).astype(o_ref.dtype)


def attention(q, k, v, sinks, *, k_ctx=None, v_ctx=None):
    b, kvh, g, seq_len, hd = q.shape
    windowed = k_ctx is not None
    q_rows = Q_BLOCK if windowed else seq_len
    nq = seq_len // q_rows
    sink_col = jnp.repeat(sinks.astype(F32), q_rows, axis=1).reshape(kvh, g * q_rows, 1)
    in_specs = [
        pl.BlockSpec((None, None, g, q_rows, hd), lambda i, h, j: (i, h, 0, j, 0)),
        pl.BlockSpec((None, None, seq_len, hd), lambda i, h, j: (i, h, 0, 0)),
        pl.BlockSpec((None, None, seq_len, hd), lambda i, h, j: (i, h, 0, 0)),
    ]
    args = [q, k, v]
    if windowed:
        s = k_ctx.shape[2]
        in_specs += [pl.BlockSpec((None, None, s, hd), lambda i, h, j: (i, h, 0, 0))] * 2
        args += [k_ctx, v_ctx]
    in_specs.append(pl.BlockSpec((None, g * q_rows, 1), lambda i, h, j: (h, 0, 0)))
    args.append(sink_col)
    return pl.pallas_call(
        functools.partial(_attn_kernel, seq_len=seq_len, q_rows=q_rows, windowed=windowed),
        out_shape=jax.ShapeDtypeStruct(q.shape, BF16),
        grid=(b, kvh, nq),
        in_specs=in_specs,
        out_specs=pl.BlockSpec((None, None, g, q_rows, hd), lambda i, h, j: (i, h, 0, j, 0)),
        compiler_params=_cparams(("arbitrary", "arbitrary", "arbitrary")),
        name="attention",
    )(*args)


def _router_kernel(h_ref, whi_ref, wlo_ref, b_ref, idx_ref, gate_ref):
    h = h_ref[...].astype(BF16)
    logits = (jnp.dot(h, whi_ref[...], preferred_element_type=F32)
              + jnp.dot(h, wlo_ref[...], preferred_element_type=F32) + b_ref[...])
    lane = lax.broadcasted_iota(jnp.int32, logits.shape, 1)
    idx_out = jnp.zeros(logits.shape, jnp.int32)
    val_out = jnp.full(logits.shape, NEG_INF, F32)
    for kk in range(TOP_K):
        m = jnp.max(logits, axis=-1, keepdims=True)
        sel = jnp.min(jnp.where(logits == m, lane, LANES), axis=-1, keepdims=True)
        idx_out = jnp.where(lane == kk, sel, idx_out)
        val_out = jnp.where(lane == kk, m, val_out)
        logits = jnp.where(lane == sel, -jnp.inf, logits)
    e = jnp.exp(val_out - jnp.max(val_out, axis=-1, keepdims=True))
    idx_ref[...] = idx_out
    gate_ref[...] = e / jnp.sum(e, axis=-1, keepdims=True)


def router(h, w_r, b_r):
    t, d = h.shape
    n_exp = w_r.shape[1]
    wp = jnp.zeros((d, LANES), F32).at[:, :n_exp].set(w_r)
    whi = wp.astype(BF16)
    wlo = (wp - whi.astype(F32)).astype(BF16)
    bp = jnp.full((1, LANES), -jnp.inf, F32).at[0, :n_exp].set(b_r)
    bm = 256
    return pl.pallas_call(
        _router_kernel,
        out_shape=[jax.ShapeDtypeStruct((t, LANES), jnp.int32), jax.ShapeDtypeStruct((t, LANES), F32)],
        grid=(t // bm,),
        in_specs=[pl.BlockSpec((bm, d), lambda i: (i, 0)), pl.BlockSpec((d, LANES), lambda i: (0, 0)),
                  pl.BlockSpec((d, LANES), lambda i: (0, 0)), pl.BlockSpec((1, LANES), lambda i: (0, 0))],
        out_specs=[pl.BlockSpec((bm, LANES), lambda i: (i, 0))] * 2,
        compiler_params=_cparams(("arbitrary",)),
        name="router",
    )(h, whi, wlo, bp)


def _gather_kernel(tok_ref, h_hbm, o_ref, buf, sem):
    n = buf.shape[0]

    def row_copy(r):
        return pltpu.make_async_copy(h_hbm.at[pl.ds(tok_ref[0, 0, r], 1), :], buf.at[pl.ds(r, 1), :], sem)

    def start(r, c):
        row_copy(r).start()
        return c

    def wait(r, c):
        row_copy(r).wait()
        return c

    lax.fori_loop(0, n, start, 0)
    lax.fori_loop(0, n, wait, 0)
    o_ref[...] = buf[...].astype(o_ref.dtype)


def gather_rows(h, slot_tok, bm):
    t, d = h.shape
    p = slot_tok.shape[0]
    return pl.pallas_call(
        _gather_kernel,
        out_shape=jax.ShapeDtypeStruct((p, d), BF16),
        grid=(p // bm,),
        in_specs=[pl.BlockSpec((1, 1, bm), lambda i: (i, 0, 0), memory_space=pltpu.SMEM),
                  pl.BlockSpec(memory_space=pl.ANY)],
        out_specs=pl.BlockSpec((bm, d), lambda i: (i, 0)),
        scratch_shapes=[pltpu.VMEM((bm, d), F32), pltpu.SemaphoreType.DMA(())],
        compiler_params=_cparams(("arbitrary",)),
        name="moe_gather",
    )(slot_tok.reshape(p // bm, 1, bm), h)


def _expert_up_kernel(be_ref, bx_ref, first_ref, act_ref, x_ref, wg_ref, wl_ref, bg_ref, bl_ref, o_ref,
                      wg_bf, wl_bf):
    r = pl.program_id(1)

    @pl.when(first_ref[r] == 1)
    def _():
        wg_bf[...] = wg_ref[...].astype(BF16)
        wl_bf[...] = wl_ref[...].astype(BF16)

    @pl.when(act_ref[r] == 1)
    def _():
        x = x_ref[...]
        glu = jnp.minimum(jnp.dot(x, wg_bf[...], preferred_element_type=F32) + bg_ref[...], SWIGLU_LIMIT)
        lin = jnp.clip(jnp.dot(x, wl_bf[...], preferred_element_type=F32) + bl_ref[...], -SWIGLU_LIMIT, SWIGLU_LIMIT)
        o_ref[...] = (glu * jax.nn.sigmoid(SWIGLU_ALPHA * glu) * (lin + 1.0)).astype(o_ref.dtype)

    @pl.when(act_ref[r] == 0)
    def _():
        o_ref[...] = jnp.zeros(o_ref.shape, o_ref.dtype)


def _expert_down_kernel(be_ref, bx_ref, first_ref, act_ref, x_ref, w_ref, b_ref, o_ref, w_bf):
    r = pl.program_id(1)

    @pl.when(first_ref[r] == 1)
    def _():
        w_bf[...] = w_ref[...].astype(BF16)

    @pl.when(act_ref[r] == 1)
    def _():
        o_ref[...] = jnp.dot(x_ref[...], w_bf[...], preferred_element_type=F32) + b_ref[...]

    @pl.when(act_ref[r] == 0)
    def _():
        o_ref[...] = jnp.zeros(o_ref.shape, o_ref.dtype)


def expert_ffn(xg, sched, layer, w_gu, b_gu, w_dn, b_dn):
    p, d = xg.shape
    nb = p // MOE_BM
    ff = w_dn.shape[2]
    tn = 512
    nt = ff // tn
    n_exp = w_gu.shape[1]
    hidden = pl.pallas_call(
        _expert_up_kernel,
        out_shape=jax.ShapeDtypeStruct((p, ff), BF16),
        grid_spec=pltpu.PrefetchScalarGridSpec(
            num_scalar_prefetch=4,
            grid=(nt, nb),
            in_specs=[
                pl.BlockSpec((MOE_BM, d), lambda t, r, be, bx, fi, ac: (bx[r], 0)),
                pl.BlockSpec((None, None, d, tn), lambda t, r, be, bx, fi, ac: (layer, be[r], 0, t)),
                pl.BlockSpec((None, None, d, tn), lambda t, r, be, bx, fi, ac: (layer, be[r], 0, nt + t)),
                pl.BlockSpec((None, None, 1, tn), lambda t, r, be, bx, fi, ac: (layer, be[r], 0, t)),
                pl.BlockSpec((None, None, 1, tn), lambda t, r, be, bx, fi, ac: (layer, be[r], 0, nt + t)),
            ],
            out_specs=pl.BlockSpec((MOE_BM, tn), lambda t, r, be, bx, fi, ac: (r, t)),
            scratch_shapes=[pltpu.VMEM((d, tn), BF16), pltpu.VMEM((d, tn), BF16)],
        ),
        compiler_params=_cparams(("arbitrary", "arbitrary")),
        name="expert_up",
    )(*sched, xg, w_gu, w_gu, b_gu.reshape(b_gu.shape[0], n_exp, 1, 2 * ff), b_gu.reshape(b_gu.shape[0], n_exp, 1, 2 * ff))
    tn2 = 1024
    return pl.pallas_call(
        _expert_down_kernel,
        out_shape=jax.ShapeDtypeStruct((p, d), F32),
        grid_spec=pltpu.PrefetchScalarGridSpec(
            num_scalar_prefetch=4,
            grid=(d // tn2, nb),
            in_specs=[
                pl.BlockSpec((MOE_BM, ff), lambda t, r, be, bx, fi, ac: (bx[r], 0)),
                pl.BlockSpec((None, None, ff, tn2), lambda t, r, be, bx, fi, ac: (layer, be[r], 0, t)),
                pl.BlockSpec((None, None, 1, tn2), lambda t, r, be, bx, fi, ac: (layer, be[r], 0, t)),
            ],
            out_specs=pl.BlockSpec((MOE_BM, tn2), lambda t, r, be, bx, fi, ac: (r, t)),
            scratch_shapes=[pltpu.VMEM((ff, tn2), BF16)],
        ),
        compiler_params=_cparams(("arbitrary", "arbitrary")),
        name="expert_down",
    )(*sched, hidden, w_dn, b_dn.reshape(b_dn.shape[0], n_exp, 1, d))


def _combine_kernel(dest_ref, yb_hbm, x_ref, gate_ref, mod_ref, o_ref, buf, sem):
    bt = x_ref.shape[0]

    def row_copy(i):
        return pltpu.make_async_copy(yb_hbm.at[pl.ds(dest_ref[0, 0, i], 1), :], buf.at[pl.ds(i, 1), :], sem)

    def start(i, c):
        row_copy(i).start()
        return c

    def wait(i, c):
        row_copy(i).wait()
        return c

    lax.fori_loop(0, TOP_K * bt, start, 0)
    lax.fori_loop(0, TOP_K * bt, wait, 0)
    g = gate_ref[...]
    y = g[:, 0:1] * buf[0:bt, :]
    for kk in range(1, TOP_K):
        y = y + g[:, kk:kk + 1] * buf[kk * bt:(kk + 1) * bt, :]
    o_ref[...] = x_ref[...] + mod_ref[...] * y


def moe_combine(x, yb, dest, gates, mods, layer, lay):
    t, d = x.shape
    bt = 128
    cond = _cond_of_block(lay, bt)
    dest_blk = dest.reshape(t // bt, bt, TOP_K).transpose(0, 2, 1).reshape(t // bt, 1, TOP_K * bt)
    return pl.pallas_call(
        _combine_kernel,
        out_shape=jax.ShapeDtypeStruct((t, d), F32),
        grid=(t // bt,),
        in_specs=[pl.BlockSpec((1, 1, TOP_K * bt), lambda i: (i, 0, 0), memory_space=pltpu.SMEM),
                  pl.BlockSpec(memory_space=pl.ANY),
                  pl.BlockSpec((bt, d), lambda i: (i, 0)),
                  pl.BlockSpec((bt, LANES), lambda i: (i, 0)),
                  pl.BlockSpec((None, None, None, 1, d), lambda i: (layer, cond(i), 5, 0, 0))],
        out_specs=pl.BlockSpec((bt, d), lambda i: (i, 0)),
        scratch_shapes=[pltpu.VMEM((TOP_K * bt, d), F32), pltpu.SemaphoreType.DMA(())],
        compiler_params=_cparams(("arbitrary",)),
        name="moe_combine",
    )(dest_blk, yb, x, gates, mods)


def _moe_schedule(top_i, n_exp):
    t = top_i.shape[0]
    n_asg = t * TOP_K
    flat_e = top_i.reshape(n_asg)
    onehot = (flat_e[:, None] == jnp.arange(n_exp, dtype=jnp.int32)[None, :]).astype(jnp.int32)
    csum = jnp.cumsum(onehot, axis=0)
    counts = csum[-1]
    pcounts = (counts + MOE_BM - 1) // MOE_BM * MOE_BM
    pends = jnp.cumsum(pcounts)
    pstarts = pends - pcounts
    dest = jnp.sum(onehot * (csum - 1 + pstarts[None, :]), axis=1)
    nb = -(-(n_asg + n_exp * (MOE_BM - 1)) // MOE_BM)
    slot_tok = jnp.zeros((nb * MOE_BM,), jnp.int32).at[dest].set(jnp.arange(n_asg, dtype=jnp.int32) // TOP_K)
    n_act = pends[-1] // MOE_BM
    blk = jnp.minimum(jnp.arange(nb, dtype=jnp.int32), n_act - 1)
    blk_e = jnp.minimum(jnp.searchsorted(pends, blk * MOE_BM, side="right"), n_exp - 1).astype(jnp.int32)
    active = (jnp.arange(nb) < n_act).astype(jnp.int32)
    first = jnp.concatenate([jnp.ones((1,), jnp.int32), (blk_e[1:] != blk_e[:-1]).astype(jnp.int32)])
    return dest.reshape(t, TOP_K), slot_tok, (blk_e, blk, first, active)


def moe_layer(x, h, layer, mods, lay, w_r, b_r, w_gu, b_gu, w_dn, b_dn):
    top_i, gates = router(h, w_r[layer], b_r[layer])
    dest, slot_tok, sched = _moe_schedule(top_i[:, :TOP_K], w_gu.shape[1])
    xg = gather_rows(h, slot_tok, MOE_BM)
    yb = expert_ffn(xg, sched, layer, w_gu, b_gu, w_dn, b_dn)
    return moe_combine(x, yb, dest, gates, mods, layer, lay)


def gla_layer(x, h, layer, idx, mods, lay, state_gla, w_in, w_gk1, w_gk2, b_gk, norm_w, w_out):
    d = h.shape[1]
    y = matmul(h, w_in, idx)
    w1 = jnp.zeros((1, d, 2 * RANK_PAD), F32)
    w1 = w1.at[0, :, :GLA_RANK].set(w_gk1[idx, 0]).at[0, :, RANK_PAD:RANK_PAD + GLA_RANK].set(w_gk1[idx, 1])
    r = matmul(h, w1, 0)
    w2 = jnp.zeros((2, RANK_PAD, GLA_QK), F32).at[:, :GLA_RANK, :].set(w_gk2[idx])
    gla_args = (r, w2, b_gk[idx].reshape(2, 1, GLA_QK), norm_w[idx].reshape(1, DV_GLA))
    common = dict(n_heads=H_GLA, dk=DK_GLA, dv=DV_GLA, gla=True, gla_args=gla_args)
    o_p, st = linear_scan(y, seq_len=lay.l_p, n_seq=lay.n_p, row_block0=0, want_state=True, **common)
    o_s, _ = linear_scan(y, seq_len=lay.l_s, n_seq=lay.n_s, row_block0=lay.t_p // lay.l_s,
                         s0=state_gla, s0_idx=idx, **common)
    o = jnp.concatenate([o_p, o_s], axis=0)
    return matmul(o, w_out, idx, resid=(x, mods, layer, 2), lay=lay), st


def ret_layer(x, h, layer, idx, mods, lay, state_ret, w_in, decay_logit, w_out):
    y = matmul(h, w_in, idx)
    common = dict(n_heads=H_RET, dk=DK_RET, dv=DV_RET, gla=False, decay_logit=decay_logit[idx])
    o_p, st = linear_scan(y, seq_len=lay.l_p, n_seq=lay.n_p, row_block0=0, want_state=True, **common)
    o_s, _ = linear_scan(y, seq_len=lay.l_s, n_seq=lay.n_s, row_block0=lay.t_p // lay.l_s,
                         rope_tabs=_rope_tables(lay.l_s, DK_RET), s0=state_ret, s0_idx=idx, **common)
    o = jnp.concatenate([o_p, o_s], axis=0)
    return matmul(o, w_out, idx, resid=(x, mods, layer, 2), lay=lay), st


def swa_layer(x, h, layer, idx, mods, lay, cache_k, cache_v, w_qkv, b_qkv, sinks, w_o, b_o):
    qkv = matmul(h, w_qkv, idx, bias=b_qkv)
    sk = sinks[idx].reshape(KV_HEADS, GROUP)

    def heads(a, n_seq, seq_len, nh):
        return a.reshape(n_seq, seq_len, nh, HD_SWA).transpose(0, 2, 1, 3)

    qkv_p = qkv[:lay.t_p]
    k_p, v_p = qkv_p[:, SWA_Q:SWA_Q + SWA_KV], qkv_p[:, SWA_Q + SWA_KV:]
    q_h = heads(qkv_p[:, :SWA_Q], lay.n_p, lay.l_p, N_HEADS_SWA).reshape(lay.n_p, KV_HEADS, GROUP, lay.l_p, HD_SWA)
    o_p = attention(q_h, heads(k_p, lay.n_p, lay.l_p, KV_HEADS), heads(v_p, lay.n_p, lay.l_p, KV_HEADS), sk)
    o_p = o_p.reshape(lay.n_p, N_HEADS_SWA, lay.l_p, HD_SWA).transpose(0, 2, 1, 3).reshape(lay.t_p, SWA_Q)

    qkv_s = qkv[lay.t_p:]
    qk_rot = rope64(qkv_s[:, :SWA_Q + SWA_KV], lay.l_s)
    q_h = heads(qk_rot[:, :SWA_Q], lay.n_s, lay.l_s, N_HEADS_SWA).reshape(lay.n_s, KV_HEADS, GROUP, lay.l_s, HD_SWA)
    k_h = heads(qk_rot[:, SWA_Q:], lay.n_s, lay.l_s, KV_HEADS)
    v_h = heads(qkv_s[:, SWA_Q + SWA_KV:], lay.n_s, lay.l_s, KV_HEADS)
    kc = cache_k[:, idx].transpose(0, 2, 1, 3)
    vc = cache_v[:, idx].transpose(0, 2, 1, 3)
    o_s = attention(q_h, k_h, v_h, sk, k_ctx=kc, v_ctx=vc)
    o_s = o_s.reshape(lay.n_s, N_HEADS_SWA, lay.l_s, HD_SWA).transpose(0, 2, 1, 3).reshape(lay.n_s * lay.l_s, SWA_Q)

    o = jnp.concatenate([o_p, o_s], axis=0)
    x_new = matmul(o, w_o, idx, bias=b_o, resid=(x, mods, layer, 2), lay=lay)
    new_k = k_p.reshape(lay.n_p, lay.l_p, KV_HEADS, HD_SWA)
    new_v = v_p.reshape(lay.n_p, lay.l_p, KV_HEADS, HD_SWA)
    return x_new, new_k, new_v


def kernel(x_prompt, x_sample, state_gla, state_ret, cache_k, cache_v, c, c_ctx, norm_mix_w, norm_ffn_w, w_ada, b_ada, gla_w_in, gla_w_gk1, gla_w_gk2, gla_b_gk, gla_norm_w, gla_w_out, ret_w_in, ret_decay_logit, ret_w_out, swa_w_qkv, swa_b_qkv, swa_sinks, swa_w_o, swa_b_o, moe_w_router, moe_b_router, moe_w_gate_up, moe_b_gate_up, moe_w_down, moe_b_down, final_norm_w):
    n_p, l_p, d = x_prompt.shape
    n_s, l_s, _ = x_sample.shape
    lay = Layout(n_p, l_p, n_s, l_s)
    depth = w_ada.shape[0]
    x = jnp.concatenate([x_prompt.reshape(n_p * l_p, d), x_sample.reshape(n_s * l_s, d)], axis=0)
    cond = jnp.zeros((COND_ROWS, d), F32).at[0].set(c_ctx).at[1:1 + n_s].set(c)
    mods = adaln_all(cond, w_ada, b_ada).reshape(depth, COND_ROWS, N_MOD, 1, d)
    nmw = norm_mix_w.reshape(depth, 1, d)
    nfw = norm_ffn_w.reshape(depth, 1, d)
    new_gla, new_ret, new_k, new_v = [], [], [], []
    for layer in range(depth):
        kind, idx = layer % N_MIXERS, layer // N_MIXERS
        h = norm_modulate(x, nmw, layer, mods, 0, lay, BF16)
        if kind == 0:
            x, st = gla_layer(x, h, layer, idx, mods, lay, state_gla, gla_w_in, gla_w_gk1, gla_w_gk2, gla_b_gk,
                              gla_norm_w, gla_w_out)
            new_gla.append(st)
        elif kind == 1:
            x, st = ret_layer(x, h, layer, idx, mods, lay, state_ret, ret_w_in, ret_decay_logit, ret_w_out)
            new_ret.append(st)
        else:
            x, kc, vc = swa_layer(x, h, layer, idx, mods, lay, cache_k, cache_v, swa_w_qkv, swa_b_qkv, swa_sinks,
                                  swa_w_o, swa_b_o)
            new_k.append(kc)
            new_v.append(vc)
        h = norm_modulate(x, nfw, layer, mods, 3, lay, F32)
        x = moe_layer(x, h, layer, mods, lay, moe_w_router, moe_b_router, moe_w_gate_up, moe_b_gate_up,
                      moe_w_down, moe_b_down)
    y = final_norm(x, final_norm_w)
    y_prompt = y[:lay.t_p].reshape(n_p, l_p, d)
    y_sample = y[lay.t_p:].reshape(n_s, l_s, d)
    return (y_prompt, y_sample, jnp.stack(new_gla, axis=1), jnp.stack(new_ret, axis=1),
            jnp.stack(new_k, axis=1), jnp.stack(new_v, axis=1))
```

```python
import functools
from typing import NamedTuple

import numpy as np
import jax
import jax.numpy as jnp
from jax import lax
from jax.experimental import pallas as pl
from jax.experimental.pallas import tpu as pltpu

F32 = jnp.float32
BF16 = jnp.bfloat16

D_MODEL = 4096
DEPTH = 4
GRID_W = 64
N_MIXERS = 3
CHUNK = 64
ROPE_BASE = 10000.0
NORM_EPS = 1e-6
N_MOD = 6
H_GLA, DK_GLA, DV_GLA = 8, 256, 512
GLA_QK = H_GLA * DK_GLA
GLA_V = H_GLA * DV_GLA
GLA_RANK = 16
GLA_GATE_NORM = 16.0
H_RET, DK_RET, DV_RET = 16, 256, 512
RET_QK = H_RET * DK_RET
RET_V = H_RET * DV_RET
N_HEADS_SWA, KV_HEADS, HD_SWA = 64, 8, 64
GROUP = N_HEADS_SWA // KV_HEADS
SWA_Q = N_HEADS_SWA * HD_SWA
SWA_KV = KV_HEADS * HD_SWA
WINDOW = 128
Q_BLOCK = 128
NEG_INF = -1e30
N_EXPERTS = 32
TOP_K = 4
D_FF = 1536
SWIGLU_ALPHA = 1.702
SWIGLU_LIMIT = 7.0

LANES = 128
VMEM_LIMIT = 56 * 1024 * 1024
COND_ROWS = 16
MOE_BM = 256
MOE_RING = 4
MOE_UP_TILE = 512
MOE_DOWN_TILE = 2048
RANK_PAD = 128
SCAN_UNROLL = 4
GATHER_PAD = 4


class Layout(NamedTuple):
    n_p: int
    l_p: int
    n_s: int
    l_s: int

    @property
    def t_p(self):
        return self.n_p * self.l_p

    @property
    def t(self):
        return self.n_p * self.l_p + self.n_s * self.l_s


def _cparams(sem):
    return pltpu.CompilerParams(dimension_semantics=sem, vmem_limit_bytes=VMEM_LIMIT)


def _cond_of_block(lay, bm):
    assert lay.t_p % bm == 0 and lay.l_s % bm == 0
    npb = lay.t_p // bm
    per = lay.l_s // bm
    return lambda i: jnp.where(i < npb, 0, 1 + (i - npb) // per)


def _pick(n, prefs):
    for p in prefs:
        if n % p == 0:
            return p
    return n


def _adaln_kernel(c_ref, w_ref, b_ref, o_ref):
    c = c_ref[...]
    s = (c * jax.nn.sigmoid(c)).astype(BF16)
    o_ref[...] = jnp.dot(s, w_ref[...].astype(BF16), preferred_element_type=F32) + b_ref[...]


def adaln_all(cond, w_ada, b_ada):
    depth, d, n = w_ada.shape
    tn = 512
    return pl.pallas_call(
        _adaln_kernel,
        out_shape=jax.ShapeDtypeStruct((depth, COND_ROWS, n), F32),
        grid=(depth, n // tn),
        in_specs=[
            pl.BlockSpec((COND_ROWS, d), lambda l, j: (0, 0)),
            pl.BlockSpec((None, d, tn), lambda l, j: (l, 0, j)),
            pl.BlockSpec((None, 1, tn), lambda l, j: (l, 0, j)),
        ],
        out_specs=pl.BlockSpec((None, COND_ROWS, tn), lambda l, j: (l, 0, j)),
        compiler_params=_cparams(("arbitrary", "arbitrary")),
        name="adaln",
    )(cond, w_ada, b_ada.reshape(depth, 1, n))


def _normmod_kernel(x_ref, nw_ref, sh_ref, sc_ref, o_ref):
    x = x_ref[...]
    y = x * lax.rsqrt(jnp.mean(x * x, axis=-1, keepdims=True) + NORM_EPS) * nw_ref[...]
    o_ref[...] = (y * (1.0 + sc_ref[...]) + sh_ref[...]).astype(o_ref.dtype)


def norm_modulate(x, norm_w, layer, mods, shift_slot, lay, out_dtype):
    t, d = x.shape
    bm = 256
    cond = _cond_of_block(lay, bm)
    return pl.pallas_call(
        _normmod_kernel,
        out_shape=jax.ShapeDtypeStruct((t, d), out_dtype),
        grid=(t // bm,),
        in_specs=[
            pl.BlockSpec((bm, d), lambda i: (i, 0)),
            pl.BlockSpec((None, 1, d), lambda i: (layer, 0, 0)),
            pl.BlockSpec((None, None, None, 1, d), lambda i: (layer, cond(i), shift_slot, 0, 0)),
            pl.BlockSpec((None, None, None, 1, d), lambda i: (layer, cond(i), shift_slot + 1, 0, 0)),
        ],
        out_specs=pl.BlockSpec((bm, d), lambda i: (i, 0)),
        compiler_params=_cparams(("arbitrary",)),
        name="norm_modulate",
    )(x, norm_w, mods, mods)


def _final_norm_kernel(x_ref, nw_ref, o_ref):
    x = x_ref[...]
    o_ref[...] = x * lax.rsqrt(jnp.mean(x * x, axis=-1, keepdims=True) + NORM_EPS) * nw_ref[...]


def final_norm(x, w):
    t, d = x.shape
    bm = 256
    return pl.pallas_call(
        _final_norm_kernel,
        out_shape=jax.ShapeDtypeStruct((t, d), F32),
        grid=(t // bm,),
        in_specs=[pl.BlockSpec((bm, d), lambda i: (i, 0)), pl.BlockSpec((1, d), lambda i: (0, 0))],
        out_specs=pl.BlockSpec((bm, d), lambda i: (i, 0)),
        compiler_params=_cparams(("arbitrary",)),
        name="final_norm",
    )(x, w.reshape(1, d))


def _mm_kernel(*refs, has_bias, resid):
    it = iter(refs)
    x_ref, w_ref = next(it), next(it)
    b_ref = next(it) if has_bias else None
    res_ref, gate_ref = (next(it), next(it)) if resid else (None, None)
    o_ref, wbf_ref = next(it), next(it)

    @pl.when(pl.program_id(1) == 0)
    def _():
        wbf_ref[...] = w_ref[...].astype(BF16)

    acc = jnp.dot(x_ref[...], wbf_ref[...], preferred_element_type=F32)
    if has_bias:
        acc = acc + b_ref[...]
    if resid:
        acc = res_ref[...] + gate_ref[...] * acc
    o_ref[...] = acc.astype(o_ref.dtype)


def matmul(x, w, idx, *, bias=None, resid=None, out_dtype=F32, lay=None):
    m, k = x.shape
    n = w.shape[-1]
    rows = m if lay is None else np.gcd(lay.t_p, lay.l_s)
    if k * 512 * 4 <= 8 * 1024 * 1024:
        bn, bm = _pick(n, (512, 256, 128)), _pick(rows, (1024, 512, 256, 128))
    else:
        bn, bm = _pick(n, (256, 128)), _pick(rows, (512, 256, 128))
    in_specs = [
        pl.BlockSpec((bm, k), lambda j, i: (i, 0)),
        pl.BlockSpec((None, k, bn), lambda j, i: (idx, 0, j)),
    ]
    args = [x, w]
    if bias is not None:
        in_specs.append(pl.BlockSpec((None, 1, bn), lambda j, i: (idx, 0, j)))
        args.append(bias.reshape(bias.shape[0], 1, n))
    if resid is not None:
        res, mods, layer, slot = resid
        cond = _cond_of_block(lay, bm)
        in_specs.append(pl.BlockSpec((bm, bn), lambda j, i: (i, j)))
        in_specs.append(pl.BlockSpec((None, None, None, 1, bn), lambda j, i: (layer, cond(i), slot, 0, j)))
        args += [res, mods]
    return pl.pallas_call(
        functools.partial(_mm_kernel, has_bias=bias is not None, resid=resid is not None),
        out_shape=jax.ShapeDtypeStruct((m, n), out_dtype),
        grid=(n // bn, m // bm),
        in_specs=in_specs,
        out_specs=pl.BlockSpec((bm, bn), lambda j, i: (i, j)),
        scratch_shapes=[pltpu.VMEM((k, bn), BF16)],
        compiler_params=_cparams(("arbitrary", "arbitrary")),
        name="matmul",
    )(*args)


def _log_sigmoid(z):
    return jnp.minimum(z, 0.0) - jnp.log1p(jnp.exp(-jnp.abs(z)))


def _split2(x):
    hi = x.astype(BF16)
    return hi, (x - hi.astype(F32)).astype(BF16)


def _dot_nt(a, b):
    return lax.dot_general(a, b, (((1,), (1,)), ((), ())), preferred_element_type=F32)


def _dot_tn(a, b):
    return lax.dot_general(a, b, (((0,), (0,)), ((), ())), preferred_element_type=F32)


def _swap_halves(x):
    parts = [pltpu.roll(x[:, o:o + LANES], LANES // 2, 1) for o in range(0, x.shape[1], LANES)]
    return parts[0] if len(parts) == 1 else jnp.concatenate(parts, axis=1)


def _scan_kernel(*refs, seq_len, dk, dv, gla, rope, has_s0, want_state):
    it = iter(refs)
    q_ref, k_ref, v_ref, g_ref = next(it), next(it), next(it), next(it)
    if gla:
        r_ref, w2_ref, bgk_ref, nw_ref = next(it), next(it), next(it), next(it)
    else:
        dl_ref = next(it)
    if rope:
        cos_ref, sin_ref = next(it), next(it)
    s0_ref = next(it) if has_s0 else None
    o_ref = next(it)
    st_ref = next(it) if want_state else None
    st_scr, oacc = next(it), next(it)
    b_scr = next(it) if gla else None

    n_chunks = seq_len // CHUNK
    half = n_chunks // 2
    head = pl.program_id(1)
    row = lax.broadcasted_iota(jnp.int32, (CHUNK, CHUNK), 0)
    col = lax.broadcasted_iota(jnp.int32, (CHUNK, CHUNK), 1)
    scale = dk ** -0.5
    keeps = [(col <= row), (col >= row)]
    if gla:
        tris = [jnp.where(kp, 1.0, 0.0).astype(BF16) for kp in keeps]

        def gate_chunk(c, carry):
            rows = pl.ds(pl.multiple_of(c * CHUNK, CHUNK), CHUNK)
            for d in range(2):
                rr = r_ref[rows, d * RANK_PAD:(d + 1) * RANK_PAD].astype(BF16)
                z = jnp.dot(rr, w2_ref[d].astype(BF16), preferred_element_type=F32) + bgk_ref[d]
                la = _log_sigmoid(z) * (1.0 / GLA_GATE_NORM)
                hi, lo = _split2(la)
                b_scr[d, rows, :] = (jnp.dot(tris[d], hi, preferred_element_type=F32)
                                     + jnp.dot(tris[d], lo, preferred_element_type=F32))
            return carry

        lax.fori_loop(0, n_chunks, gate_chunk, 0, unroll=2)
    else:
        ret_f = []
        for d in range(2):
            la_c = _log_sigmoid(jnp.full((CHUNK, dk), dl_ref[d, head], F32))
            pos = lax.broadcasted_iota(jnp.int32, (CHUNK, dk), 0)
            b_c = (pos + 1 if d == 0 else CHUNK - pos).astype(F32) * la_c
            b_end_c = float(CHUNK) * la_c
            dec_row = jnp.exp(float(CHUNK) * _log_sigmoid(jnp.full((1, dk), dl_ref[d, head], F32)))
            ret_f.append((jnp.exp(b_c), jnp.exp(-b_c) * scale, jnp.exp(b_end_c - b_c) * scale, dec_row))
    for d in range(2):
        if has_s0:
            st_scr[d] = s0_ref[d].T
        else:
            st_scr[d] = jnp.zeros((dv, dk), F32)

    def chunk_out(d, c):
        rows = pl.ds(pl.multiple_of(c * CHUNK, CHUNK), CHUNK)
        qc, kc = q_ref[rows, :], k_ref[rows, :]
        vc = v_ref[rows, :].astype(BF16)
        if rope:
            cs, sn = cos_ref[rows, :], sin_ref[rows, :]
            qc = qc * cs + _swap_halves(qc) * sn
            kc = kc * cs + _swap_halves(kc) * sn
        if gla:
            b = b_scr[d, rows, :]
            last = c * CHUNK + (CHUNK - 1 if d == 0 else 0)
            b_end = b_scr[d, pl.ds(last, 1), :]
            q_dec = (qc * (jnp.exp(b) * scale)).astype(BF16)
            k_inv = (kc * jnp.exp(-b)).astype(BF16)
            k_end = (kc * jnp.exp(b_end - b)).astype(BF16)
            dec = jnp.exp(b_end)
        else:
            e_q, e_inv, e_end, dec = ret_f[d]
            q_dec = (qc * e_q).astype(BF16)
            k_inv = (kc * e_inv).astype(BF16)
            k_end = (kc * e_end).astype(BF16)
        scores = jnp.where(keeps[d], _dot_nt(q_dec, k_inv), 0.0).astype(BF16)
        st = st_scr[d]
        o = jnp.dot(scores, vc, preferred_element_type=F32) + _dot_nt(q_dec, st.astype(BF16))
        st_scr[d] = st * dec + _dot_tn(vc, k_end)
        return rows, o

    def finish(rows, o):
        tot = oacc[rows, :] + o
        y = tot * lax.rsqrt(jnp.mean(tot * tot, axis=-1, keepdims=True) + NORM_EPS)
        if gla:
            y = y * nw_ref[...]
        gg = g_ref[rows, :]
        o_ref[rows, :] = (y * (gg * jax.nn.sigmoid(gg))).astype(o_ref.dtype)

    def first_visit(i, carry):
        rf, of = chunk_out(0, i)
        rb, ob = chunk_out(1, n_chunks - 1 - i)
        oacc[rf, :] = of
        oacc[rb, :] = ob
        return carry

    def second_visit(i, carry):
        rf, of = chunk_out(0, i)
        rb, ob = chunk_out(1, n_chunks - 1 - i)
        finish(rf, of)
        finish(rb, ob)
        return carry

    lax.fori_loop(0, half, first_visit, 0, unroll=SCAN_UNROLL)
    lax.fori_loop(half, n_chunks, second_visit, 0, unroll=SCAN_UNROLL)
    if want_state:
        for d in range(2):
            st_ref[d] = st_scr[d].T


def linear_scan(y, *, seq_len, n_seq, row_block0, n_heads, dk, dv, gla, rope_tabs=None, s0=None, s0_idx=0,
                want_state=False, gla_args=None, decay_logit=None):
    assert seq_len % (2 * CHUNK) == 0
    qk = n_heads * dk
    kb, vb, gb = qk // dk, 2 * qk // dv, (2 * qk + n_heads * dv) // dv
    in_specs = [
        pl.BlockSpec((seq_len, dk), lambda b, h: (row_block0 + b, h)),
        pl.BlockSpec((seq_len, dk), lambda b, h: (row_block0 + b, kb + h)),
        pl.BlockSpec((seq_len, dv), lambda b, h: (row_block0 + b, vb + h)),
        pl.BlockSpec((seq_len, dv), lambda b, h: (row_block0 + b, gb + h)),
    ]
    args = [y, y, y, y]
    if gla:
        r, w2, bgk, nw = gla_args
        in_specs += [
            pl.BlockSpec((seq_len, 2 * RANK_PAD), lambda b, h: (row_block0 + b, 0)),
            pl.BlockSpec((2, RANK_PAD, dk), lambda b, h: (0, 0, h)),
            pl.BlockSpec((2, 1, dk), lambda b, h: (0, 0, h)),
            pl.BlockSpec((1, dv), lambda b, h: (0, 0)),
        ]
        args += [r, w2, bgk, nw]
    else:
        in_specs.append(pl.BlockSpec(memory_space=pltpu.SMEM))
        args.append(decay_logit)
    if rope_tabs is not None:
        in_specs += [pl.BlockSpec((seq_len, dk), lambda b, h: (0, 0))] * 2
        args += list(rope_tabs)
    if s0 is not None:
        in_specs.append(pl.BlockSpec((None, None, 2, None, dk, dv), lambda b, h: (b, s0_idx, 0, h, 0, 0)))
        args.append(s0)
    out_shape = [jax.ShapeDtypeStruct((n_seq * seq_len, n_heads * dv), BF16)]
    out_specs = [pl.BlockSpec((seq_len, dv), lambda b, h: (b, h))]
    if want_state:
        out_shape.append(jax.ShapeDtypeStruct((n_seq, 2, n_heads, dk, dv), F32))
        out_specs.append(pl.BlockSpec((None, 2, None, dk, dv), lambda b, h: (b, 0, h, 0, 0)))
    outs = pl.pallas_call(
        functools.partial(_scan_kernel, seq_len=seq_len, dk=dk, dv=dv, gla=gla, rope=rope_tabs is not None,
                          has_s0=s0 is not None, want_state=want_state),
        out_shape=out_shape,
        grid=(n_seq, n_heads),
        in_specs=in_specs,
        out_specs=out_specs,
        scratch_shapes=[pltpu.VMEM((2, dv, dk), F32), pltpu.VMEM((seq_len, dv), F32)]
        + ([pltpu.VMEM((2, seq_len, dk), F32)] if gla else []),
        compiler_params=_cparams(("arbitrary", "arbitrary")),
        name="linear_scan",
    )(*args)
    return outs[0], (outs[1] if want_state else None)


def _rope_tables(seq_len, d):
    half, nf = d // 2, d // 4
    pos = np.arange(seq_len)
    inv = ROPE_BASE ** (-np.arange(nf, dtype=np.float32) / nf)
    cos_l, sin_l = [], []
    for p in (pos // GRID_W, pos % GRID_W):
        ang = p.astype(np.float32)[:, None] * inv[None, :]
        c, s = np.cos(ang), np.sin(ang)
        cos_l += [c, c]
        sin_l += [-s, s]
    return (jnp.asarray(np.concatenate(cos_l, axis=1), F32), jnp.asarray(np.concatenate(sin_l, axis=1), F32))


def _rope64_kernel(x_ref, cos_ref, s1_ref, s2_ref, o_ref):
    cs, s1, s2 = cos_ref[...], s1_ref[...], s2_ref[...]
    for o in range(0, x_ref.shape[1], LANES):
        x = x_ref[:, o:o + LANES]
        o_ref[:, o:o + LANES] = (x * cs + pltpu.roll(x, LANES - HD_SWA // 4, 1) * s1
                                 + pltpu.roll(x, HD_SWA // 4, 1) * s2)


def rope64(x, seq_len):
    t, w = x.shape
    nf = HD_SWA // 4
    pos = np.arange(seq_len)
    inv = ROPE_BASE ** (-np.arange(nf, dtype=np.float32) / nf)
    ar = (pos // GRID_W).astype(np.float32)[:, None] * inv[None, :]
    ac = (pos % GRID_W).astype(np.float32)[:, None] * inv[None, :]
    zero = np.zeros_like(ar)
    cos = np.concatenate([np.cos(ar), np.cos(ar), np.cos(ac), np.cos(ac)], axis=1)
    s1 = np.concatenate([-np.sin(ar), zero, -np.sin(ac), zero], axis=1)
    s2 = np.concatenate([zero, np.sin(ar), zero, np.sin(ac)], axis=1)
    tabs = [jnp.asarray(np.tile(a, (1, LANES // HD_SWA)), F32) for a in (cos, s1, s2)]
    bm = _pick(seq_len, (256, 128))
    per = seq_len // bm
    return pl.pallas_call(
        _rope64_kernel,
        out_shape=jax.ShapeDtypeStruct((t, w), F32),
        grid=(t // bm,),
        in_specs=[pl.BlockSpec((bm, w), lambda i: (i, 0))] + [pl.BlockSpec((bm, LANES), lambda i: (i % per, 0))] * 3,
        out_specs=pl.BlockSpec((bm, w), lambda i: (i, 0)),
        compiler_params=_cparams(("arbitrary",)),
        name="rope64",
    )(x, *tabs)


def _attn_kernel(*refs, seq_len, q_rows, windowed):
    it = iter(refs)
    q_ref, k_ref, v_ref = next(it), next(it), next(it)
    kc_ref, vc_ref = (next(it), next(it)) if windowed else (None, None)
    sink_ref, o_ref = next(it), next(it)
    rows = GROUP * q_rows
    scale = HD_SWA ** -0.5
    q = q_ref[...].reshape(rows, HD_SWA).astype(BF16)
    sink = sink_ref[...]
    if windowed:
        j = pl.program_id(2)
        span = 3 * Q_BLOCK
        ws = pl.multiple_of(jnp.clip(j * Q_BLOCK - Q_BLOCK, 0, seq_len - span), Q_BLOCK)
        kw = k_ref[pl.ds(ws, span), :].astype(BF16)
        vw = v_ref[pl.ds(ws, span), :].astype(BF16)
        qpos = j * Q_BLOCK + (lax.broadcasted_iota(jnp.int32, (rows, span), 0) & (q_rows - 1))
        kpos = ws + lax.broadcasted_iota(jnp.int32, (rows, span), 1)
        s1 = jnp.where(jnp.abs(qpos - kpos) <= WINDOW, _dot_nt(q, kw) * scale, NEG_INF)
        s2 = _dot_nt(q, kc_ref[...].astype(BF16)) * scale
        m = jnp.maximum(jnp.maximum(jnp.max(s1, axis=-1, keepdims=True), jnp.max(s2, axis=-1, keepdims=True)), sink)
        p1, p2 = jnp.exp(s1 - m), jnp.exp(s2 - m)
        den = jnp.sum(p1, axis=-1, keepdims=True) + jnp.sum(p2, axis=-1, keepdims=True) + jnp.exp(sink - m)
        o = (jnp.dot(p1.astype(BF16), vw, preferred_element_type=F32)
             + jnp.dot(p2.astype(BF16), vc_ref[...].astype(BF16), preferred_element_type=F32))
    else:
        s1 = _dot_nt(q, k_ref[...].astype(BF16)) * scale
        m = jnp.maximum(jnp.max(s1, axis=-1, keepdims=True), sink)
        p1 = jnp.exp(s1 - m)
        den = jnp.sum(p1, axis=-1, keepdims=True) + jnp.exp(sink - m)
        o = jnp.dot(p1.astype(BF16), v_ref[...].astype(BF16), preferred_element_type=F32)
    o_ref[...] = (o / den).reshape(GROUP, q_rows, HD_SWA).astype(o_ref.dtype)


def attention(q, k, v, sinks, *, k_ctx=None, v_ctx=None):
    b, kvh, g, seq_len, hd = q.shape
    windowed = k_ctx is not None
    q_rows = Q_BLOCK if windowed else seq_len
    nq = seq_len // q_rows
    sink_col = jnp.repeat(sinks.astype(F32), q_rows, axis=1).reshape(kvh, g * q_rows, 1)
    in_specs = [
        pl.BlockSpec((None, None, g, q_rows, hd), lambda i, h, j: (i, h, 0, j, 0)),
        pl.BlockSpec((None, None, seq_len, hd), lambda i, h, j: (i, h, 0, 0)),
        pl.BlockSpec((None, None, seq_len, hd), lambda i, h, j: (i, h, 0, 0)),
    ]
    args = [q, k, v]
    if windowed:
        s = k_ctx.shape[2]
        in_specs += [pl.BlockSpec((None, None, s, hd), lambda i, h, j: (i, h, 0, 0))] * 2
        args += [k_ctx, v_ctx]
    in_specs.append(pl.BlockSpec((None, g * q_rows, 1), lambda i, h, j: (h, 0, 0)))
    args.append(sink_col)
    return pl.pallas_call(
        functools.partial(_attn_kernel, seq_len=seq_len, q_rows=q_rows, windowed=windowed),
        out_shape=jax.ShapeDtypeStruct(q.shape, BF16),
        grid=(b, kvh, nq),
        in_specs=in_specs,
        out_specs=pl.BlockSpec((None, None, g, q_rows, hd), lambda i, h, j: (i, h, 0, j, 0)),
        compiler_params=_cparams(("arbitrary", "arbitrary", "arbitrary")),
        name="attention",
    )(*args)


def _router_kernel(h_ref, whi_ref, wlo_ref, b_ref, idx_ref, gate_ref):
    h = h_ref[...].astype(BF16)
    logits = (jnp.dot(h, whi_ref[...], preferred_element_type=F32)
              + jnp.dot(h, wlo_ref[...], preferred_element_type=F32) + b_ref[...])
    lane = lax.broadcasted_iota(jnp.int32, logits.shape, 1)
    idx_out = jnp.zeros(logits.shape, jnp.int32)
    val_out = jnp.full(logits.shape, NEG_INF, F32)
    for kk in range(TOP_K):
        m = jnp.max(logits, axis=-1, keepdims=True)
        sel = jnp.min(jnp.where(logits == m, lane, LANES), axis=-1, keepdims=True)
        idx_out = jnp.where(lane == kk, sel, idx_out)
        val_out = jnp.where(lane == kk, m, val_out)
        logits = jnp.where(lane == sel, -jnp.inf, logits)
    e = jnp.exp(val_out - jnp.max(val_out, axis=-1, keepdims=True))
    idx_ref[...] = idx_out
    gate_ref[...] = e / jnp.sum(e, axis=-1, keepdims=True)


def router(h, w_r, b_r):
    t, d = h.shape
    n_exp = w_r.shape[1]
    wp = jnp.zeros((d, LANES), F32).at[:, :n_exp].set(w_r)
    whi = wp.astype(BF16)
    wlo = (wp - whi.astype(F32)).astype(BF16)
    bp = jnp.full((1, LANES), -jnp.inf, F32).at[0, :n_exp].set(b_r)
    bm = 256
    return pl.pallas_call(
        _router_kernel,
        out_shape=[jax.ShapeDtypeStruct((t, LANES), jnp.int32), jax.ShapeDtypeStruct((t, LANES), F32)],
        grid=(t // bm,),
        in_specs=[pl.BlockSpec((bm, d), lambda i: (i, 0)), pl.BlockSpec((d, LANES), lambda i: (0, 0)),
                  pl.BlockSpec((d, LANES), lambda i: (0, 0)), pl.BlockSpec((1, LANES), lambda i: (0, 0))],
        out_specs=[pl.BlockSpec((bm, LANES), lambda i: (i, 0))] * 2,
        compiler_params=_cparams(("arbitrary",)),
        name="router",
    )(h, whi, wlo, bp)


def _gather_kernel(tok_ref, h_hbm, o_ref, buf, sem, *, sub, pitch):
    i = pl.program_id(0)
    n_steps = pl.num_programs(0)
    bm = o_ref.shape[0]

    def row_copy(step, r, slot):
        tok = tok_ref[step * bm + r]
        return pltpu.make_async_copy(h_hbm.at[pl.ds(pl.multiple_of(tok * sub, sub), sub), :],
                                     buf.at[slot, pl.ds(pl.multiple_of(r * pitch, GATHER_PAD), sub), :], sem.at[slot])

    def start_rows(step, slot):
        def body(r, c):
            row_copy(step, r, slot).start()
            return c
        lax.fori_loop(0, bm, body, 0, unroll=8)

    def finish_rows(step, slot):
        def body(r, c):
            row_copy(step, r, slot).wait()
            return c
        lax.fori_loop(0, bm, body, 0, unroll=8)
        for s in range(sub):
            o_ref[:, s * LANES:(s + 1) * LANES] = buf[slot, pl.ds(s, bm, stride=pitch), :].astype(o_ref.dtype)

    @pl.when(i == 0)
    def _():
        start_rows(0, 0)

    for slot in range(2):
        @pl.when(lax.rem(i, 2) == slot)
        def _(slot=slot):
            @pl.when(i + 1 < n_steps)
            def _():
                start_rows(i + 1, 1 - slot)
            finish_rows(i, slot)


def gather_rows(h, slot_tok, bm):
    t, d = h.shape
    p = slot_tok.shape[0]
    sub = d // LANES
    pitch = sub + GATHER_PAD
    h_rows = h.reshape(t * sub, LANES)
    return pl.pallas_call(
        functools.partial(_gather_kernel, sub=sub, pitch=pitch),
        out_shape=jax.ShapeDtypeStruct((p, d), BF16),
        grid_spec=pltpu.PrefetchScalarGridSpec(
            num_scalar_prefetch=1,
            grid=(p // bm,),
            in_specs=[pl.BlockSpec(memory_space=pl.ANY)],
            out_specs=pl.BlockSpec((bm, d), lambda i, tok: (i, 0)),
            scratch_shapes=[pltpu.VMEM((2, bm * pitch, LANES), F32), pltpu.SemaphoreType.DMA((2,))],
        ),
        compiler_params=_cparams(("arbitrary",)),
        name="moe_gather",
    )(slot_tok, h_rows)


def _expert_rows(n, base, n_act, nb_total, x_hbm, o_hbm, xbuf, obuf, zbuf, xsem, osem, zsem, tn, compute):
    t, e = pl.program_id(0), pl.program_id(1)
    bm = xbuf.shape[1]
    look = MOE_RING - 1
    total = pl.num_programs(0) * n_act
    last_expert = e == pl.num_programs(1) - 1

    def x_copy(cnt):
        rows = pl.ds(pl.multiple_of(lax.rem(cnt, n_act) * bm, bm), bm)
        slot = lax.rem(cnt, MOE_RING)
        return pltpu.make_async_copy(x_hbm.at[rows, :], xbuf.at[slot], xsem.at[slot])

    def o_copy(cnt):
        rows = pl.ds(pl.multiple_of(lax.rem(cnt, n_act) * bm, bm), bm)
        cols = pl.ds(pl.multiple_of(lax.div(cnt, n_act) * tn, tn), tn)
        slot = lax.rem(cnt, 2)
        return pltpu.make_async_copy(obuf.at[slot], o_hbm.at[rows, cols], osem.at[slot])

    @pl.when(jnp.logical_and(t == 0, e == 0))
    def _():
        for c in range(look):
            @pl.when(c < total)
            def _(c=c):
                x_copy(jnp.int32(c)).start()

    cnt0 = t * n_act + base

    def visit(j, carry):
        cnt = cnt0 + j
        x_copy(cnt).wait()

        @pl.when(cnt + look < total)
        def _():
            x_copy(cnt + look).start()

        res = compute(xbuf[lax.rem(cnt, MOE_RING)])

        @pl.when(cnt >= 2)
        def _():
            o_copy(cnt - 2).wait()

        obuf[lax.rem(cnt, 2)] = res
        o_copy(cnt).start()
        return carry

    lax.fori_loop(0, n, visit, 0)

    @pl.when(jnp.logical_and(t == pl.num_programs(0) - 1, last_expert))
    def _():
        for back in (2, 1):
            @pl.when(total >= back)
            def _(back=back):
                o_copy(total - back).wait()

    @pl.when(jnp.logical_and(last_expert, n_act < nb_total))
    def _():
        zbuf[...] = jnp.zeros(zbuf.shape, zbuf.dtype)
        cols = pl.ds(pl.multiple_of(t * tn, tn), tn)

        def tail(blk, carry):
            cp = pltpu.make_async_copy(zbuf, o_hbm.at[pl.ds(pl.multiple_of(blk * bm, bm), bm), cols], zsem)
            cp.start()
            cp.wait()
            return carry

        lax.fori_loop(n_act, nb_total, tail, 0)


def _expert_up_kernel(nblk_ref, bstart_ref, nact_ref, x_hbm, wg_ref, wl_ref, bg_ref, bl_ref, o_hbm,
                      wg_bf, wl_bf, xbuf, obuf, zbuf, xsem, osem, zsem, *, nb_total):
    e = pl.program_id(1)
    n = nblk_ref[e]

    @pl.when(n > 0)
    def _():
        wg_bf[...] = wg_ref[...].astype(BF16)
        wl_bf[...] = wl_ref[...].astype(BF16)

    def compute(x):
        glu = jnp.minimum(jnp.dot(x, wg_bf[...], preferred_element_type=F32) + bg_ref[...], SWIGLU_LIMIT)
        lin = jnp.clip(jnp.dot(x, wl_bf[...], preferred_element_type=F32) + bl_ref[...], -SWIGLU_LIMIT, SWIGLU_LIMIT)
        return (glu * jax.nn.sigmoid(SWIGLU_ALPHA * glu) * (lin + 1.0)).astype(obuf.dtype)

    _expert_rows(n, bstart_ref[e], nact_ref[0], nb_total, x_hbm, o_hbm, xbuf, obuf, zbuf, xsem, osem, zsem,
                 wg_bf.shape[1], compute)


def _expert_down_kernel(nblk_ref, bstart_ref, nact_ref, x_hbm, w_ref, b_ref, o_hbm,
                        w_bf, xbuf, obuf, zbuf, xsem, osem, zsem, *, nb_total):
    e = pl.program_id(1)
    n = nblk_ref[e]

    @pl.when(n > 0)
    def _():
        w_bf[...] = w_ref[...].astype(BF16)

    def compute(x):
        return jnp.dot(x, w_bf[...], preferred_element_type=F32) + b_ref[...]

    _expert_rows(n, bstart_ref[e], nact_ref[0], nb_total, x_hbm, o_hbm, xbuf, obuf, zbuf, xsem, osem, zsem,
                 w_bf.shape[1], compute)


def expert_ffn(xg, sched, layer, w_gu, b_gu, w_dn, b_dn):
    p, d = xg.shape
    nb = p // MOE_BM
    ff = w_dn.shape[2]
    tn, tn2 = MOE_UP_TILE, MOE_DOWN_TILE
    nt = ff // tn
    n_exp = w_gu.shape[1]
    bgu = b_gu.reshape(b_gu.shape[0], n_exp, 1, 2 * ff)
    dma = pltpu.SemaphoreType.DMA
    hidden = pl.pallas_call(
        functools.partial(_expert_up_kernel, nb_total=nb),
        out_shape=jax.ShapeDtypeStruct((p, ff), BF16),
        grid_spec=pltpu.PrefetchScalarGridSpec(
            num_scalar_prefetch=3,
            grid=(nt, n_exp),
            in_specs=[
                pl.BlockSpec(memory_space=pl.ANY),
                pl.BlockSpec((None, None, d, tn), lambda t, e, *_: (layer, e, 0, t)),
                pl.BlockSpec((None, None, d, tn), lambda t, e, *_: (layer, e, 0, nt + t)),
                pl.BlockSpec((None, None, 1, tn), lambda t, e, *_: (layer, e, 0, t)),
                pl.BlockSpec((None, None, 1, tn), lambda t, e, *_: (layer, e, 0, nt + t)),
            ],
            out_specs=pl.BlockSpec(memory_space=pl.ANY),
            scratch_shapes=[pltpu.VMEM((d, tn), BF16), pltpu.VMEM((d, tn), BF16),
                            pltpu.VMEM((MOE_RING, MOE_BM, d), BF16), pltpu.VMEM((2, MOE_BM, tn), BF16),
                            pltpu.VMEM((MOE_BM, tn), BF16), dma((MOE_RING,)), dma((2,)), dma(())],
        ),
        compiler_params=_cparams(("arbitrary", "arbitrary")),
        name="expert_up",
    )(*sched, xg, w_gu, w_gu, bgu, bgu)
    return pl.pallas_call(
        functools.partial(_expert_down_kernel, nb_total=nb),
        out_shape=jax.ShapeDtypeStruct((p, d), F32),
        grid_spec=pltpu.PrefetchScalarGridSpec(
            num_scalar_prefetch=3,
            grid=(d // tn2, n_exp),
            in_specs=[
                pl.BlockSpec(memory_space=pl.ANY),
                pl.BlockSpec((None, None, ff, tn2), lambda t, e, *_: (layer, e, 0, t)),
                pl.BlockSpec((None, None, 1, tn2), lambda t, e, *_: (layer, e, 0, t)),
            ],
            out_specs=pl.BlockSpec(memory_space=pl.ANY),
            scratch_shapes=[pltpu.VMEM((ff, tn2), BF16),
                            pltpu.VMEM((MOE_RING, MOE_BM, ff), BF16), pltpu.VMEM((2, MOE_BM, tn2), F32),
                            pltpu.VMEM((MOE_BM, tn2), F32), dma((MOE_RING,)), dma((2,)), dma(())],
        ),
        compiler_params=_cparams(("arbitrary", "arbitrary")),
        name="expert_down",
    )(*sched, hidden, w_dn, b_dn.reshape(b_dn.shape[0], n_exp, 1, d))


def _combine_kernel(dest_ref, yb_hbm, x_ref, gate_ref, mod_ref, o_ref, buf, sem):
    bt = x_ref.shape[0]

    def row_copy(i):
        return pltpu.make_async_copy(yb_hbm.at[pl.ds(dest_ref[0, 0, i], 1), :], buf.at[pl.ds(i, 1), :], sem)

    def start(i, c):
        row_copy(i).start()
        return c

    def wait(i, c):
        row_copy(i).wait()
        return c

    lax.fori_loop(0, TOP_K * bt, start, 0)
    lax.fori_loop(0, TOP_K * bt, wait, 0)
    g = gate_ref[...]
    y = g[:, 0:1] * buf[0:bt, :]
    for kk in range(1, TOP_K):
        y = y + g[:, kk:kk + 1] * buf[kk * bt:(kk + 1) * bt, :]
    o_ref[...] = x_ref[...] + mod_ref[...] * y


def moe_combine(x, yb, dest, gates, mods, layer, lay):
    t, d = x.shape
    bt = 128
    cond = _cond_of_block(lay, bt)
    dest_blk = dest.reshape(t // bt, bt, TOP_K).transpose(0, 2, 1).reshape(t // bt, 1, TOP_K * bt)
    return pl.pallas_call(
        _combine_kernel,
        out_shape=jax.ShapeDtypeStruct((t, d), F32),
        grid=(t // bt,),
        in_specs=[pl.BlockSpec((1, 1, TOP_K * bt), lambda i: (i, 0, 0), memory_space=pltpu.SMEM),
                  pl.BlockSpec(memory_space=pl.ANY),
                  pl.BlockSpec((bt, d), lambda i: (i, 0)),
                  pl.BlockSpec((bt, LANES), lambda i: (i, 0)),
                  pl.BlockSpec((None, None, None, 1, d), lambda i: (layer, cond(i), 5, 0, 0))],
        out_specs=pl.BlockSpec((bt, d), lambda i: (i, 0)),
        scratch_shapes=[pltpu.VMEM((TOP_K * bt, d), F32), pltpu.SemaphoreType.DMA(())],
        compiler_params=_cparams(("arbitrary",)),
        name="moe_combine",
    )(dest_blk, yb, x, gates, mods)


def _moe_schedule(top_i, n_exp):
    t = top_i.shape[0]
    n_asg = t * TOP_K
    flat_e = top_i.reshape(n_asg)
    onehot = (flat_e[:, None] == jnp.arange(n_exp, dtype=jnp.int32)[None, :]).astype(jnp.int32)
    oh = onehot.astype(F32).reshape(n_asg // LANES, LANES, n_exp)
    within = jnp.einsum("ij,bjk->bik", jnp.tril(jnp.ones((LANES, LANES), F32)), oh)
    totals = within[:, -1, :]
    csum = (within + (jnp.cumsum(totals, axis=0) - totals)[:, None, :]).reshape(n_asg, n_exp).astype(jnp.int32)
    counts = csum[-1]
    pcounts = (counts + MOE_BM - 1) // MOE_BM * MOE_BM
    pends = jnp.cumsum(pcounts)
    pstarts = pends - pcounts
    dest = jnp.sum(onehot * (csum - 1 + pstarts[None, :]), axis=1)
    nb = -(-(n_asg + n_exp * (MOE_BM - 1)) // MOE_BM)
    slot_tok = jnp.zeros((nb * MOE_BM,), jnp.int32).at[dest].set(jnp.arange(n_asg, dtype=jnp.int32) // TOP_K)
    nblk = (pcounts // MOE_BM).astype(jnp.int32)
    bstart = (pstarts // MOE_BM).astype(jnp.int32)
    n_act = (pends[-1:] // MOE_BM).astype(jnp.int32)
    return dest.reshape(t, TOP_K), slot_tok, (nblk, bstart, n_act)


def moe_layer(x, h, layer, mods, lay, w_r, b_r, w_gu, b_gu, w_dn, b_dn):
    top_i, gates = router(h, w_r[layer], b_r[layer])
    dest, slot_tok, sched = _moe_schedule(top_i[:, :TOP_K], w_gu.shape[1])
    xg = gather_rows(h, slot_tok, MOE_BM)
    yb = expert_ffn(xg, sched, layer, w_gu, b_gu, w_dn, b_dn)
    return moe_combine(x, yb, dest, gates, mods, layer, lay)


def gla_layer(x, h, layer, idx, mods, lay, state_gla, w_in, w_gk1, w_gk2, b_gk, norm_w, w_out):
    d = h.shape[1]
    y = matmul(h, w_in, idx)
    w1 = jnp.zeros((1, d, 2 * RANK_PAD), F32)
    w1 = w1.at[0, :, :GLA_RANK].set(w_gk1[idx, 0]).at[0, :, RANK_PAD:RANK_PAD + GLA_RANK].set(w_gk1[idx, 1])
    r = matmul(h, w1, 0)
    w2 = jnp.zeros((2, RANK_PAD, GLA_QK), F32).at[:, :GLA_RANK, :].set(w_gk2[idx])
    gla_args = (r, w2, b_gk[idx].reshape(2, 1, GLA_QK), norm_w[idx].reshape(1, DV_GLA))
    common = dict(n_heads=H_GLA, dk=DK_GLA, dv=DV_GLA, gla=True, gla_args=gla_args)
    o_p, st = linear_scan(y, seq_len=lay.l_p, n_seq=lay.n_p, row_block0=0, want_state=True, **common)
    o_s, _ = linear_scan(y, seq_len=lay.l_s, n_seq=lay.n_s, row_block0=lay.t_p // lay.l_s,
                         s0=state_gla, s0_idx=idx, **common)
    o = jnp.concatenate([o_p, o_s], axis=0)
    return matmul(o, w_out, idx, resid=(x, mods, layer, 2), lay=lay), st


def ret_layer(x, h, layer, idx, mods, lay, state_ret, w_in, decay_logit, w_out):
    y = matmul(h, w_in, idx)
    common = dict(n_heads=H_RET, dk=DK_RET, dv=DV_RET, gla=False, decay_logit=decay_logit[idx])
    o_p, st = linear_scan(y, seq_len=lay.l_p, n_seq=lay.n_p, row_block0=0, want_state=True, **common)
    o_s, _ = linear_scan(y, seq_len=lay.l_s, n_seq=lay.n_s, row_block0=lay.t_p // lay.l_s,
                         rope_tabs=_rope_tables(lay.l_s, DK_RET), s0=state_ret, s0_idx=idx, **common)
    o = jnp.concatenate([o_p, o_s], axis=0)
    return matmul(o, w_out, idx, resid=(x, mods, layer, 2), lay=lay), st


def swa_layer(x, h, layer, idx, mods, lay, cache_k, cache_v, w_qkv, b_qkv, sinks, w_o, b_o):
    qkv = matmul(h, w_qkv, idx, bias=b_qkv)
    sk = sinks[idx].reshape(KV_HEADS, GROUP)

    def heads(a, n_seq, seq_len, nh):
        return a.reshape(n_seq, seq_len, nh, HD_SWA).transpose(0, 2, 1, 3)

    qkv_p = qkv[:lay.t_p]
    k_p, v_p = qkv_p[:, SWA_Q:SWA_Q + SWA_KV], qkv_p[:, SWA_Q + SWA_KV:]
    q_h = heads(qkv_p[:, :SWA_Q], lay.n_p, lay.l_p, N_HEADS_SWA).reshape(lay.n_p, KV_HEADS, GROUP, lay.l_p, HD_SWA)
    o_p = attention(q_h, heads(k_p, lay.n_p, lay.l_p, KV_HEADS), heads(v_p, lay.n_p, lay.l_p, KV_HEADS), sk)
    o_p = o_p.reshape(lay.n_p, N_HEADS_SWA, lay.l_p, HD_SWA).transpose(0, 2, 1, 3).reshape(lay.t_p, SWA_Q)

    qkv_s = qkv[lay.t_p:]
    qk_rot = rope64(qkv_s[:, :SWA_Q + SWA_KV], lay.l_s)
    q_h = heads(qk_rot[:, :SWA_Q], lay.n_s, lay.l_s, N_HEADS_SWA).reshape(lay.n_s, KV_HEADS, GROUP, lay.l_s, HD_SWA)
    k_h = heads(qk_rot[:, SWA_Q:], lay.n_s, lay.l_s, KV_HEADS)
    v_h = heads(qkv_s[:, SWA_Q + SWA_KV:], lay.n_s, lay.l_s, KV_HEADS)
    kc = cache_k[:, idx].transpose(0, 2, 1, 3)
    vc = cache_v[:, idx].transpose(0, 2, 1, 3)
    o_s = attention(q_h, k_h, v_h, sk, k_ctx=kc, v_ctx=vc)
    o_s = o_s.reshape(lay.n_s, N_HEADS_SWA, lay.l_s, HD_SWA).transpose(0, 2, 1, 3).reshape(lay.n_s * lay.l_s, SWA_Q)

    o = jnp.concatenate([o_p, o_s], axis=0)
    x_new = matmul(o, w_o, idx, bias=b_o, resid=(x, mods, layer, 2), lay=lay)
    new_k = k_p.reshape(lay.n_p, lay.l_p, KV_HEADS, HD_SWA)
    new_v = v_p.reshape(lay.n_p, lay.l_p, KV_HEADS, HD_SWA)
    return x_new, new_k, new_v


def kernel(x_prompt, x_sample, state_gla, state_ret, cache_k, cache_v, c, c_ctx, norm_mix_w, norm_ffn_w, w_ada, b_ada, gla_w_in, gla_w_gk1, gla_w_gk2, gla_b_gk, gla_norm_w, gla_w_out, ret_w_in, ret_decay_logit, ret_w_out, swa_w_qkv, swa_b_qkv, swa_sinks, swa_w_o, swa_b_o, moe_w_router, moe_b_router, moe_w_gate_up, moe_b_gate_up, moe_w_down, moe_b_down, final_norm_w):
    n_p, l_p, d = x_prompt.shape
    n_s, l_s, _ = x_sample.shape
    lay = Layout(n_p, l_p, n_s, l_s)
    depth = w_ada.shape[0]
    x = jnp.concatenate([x_prompt.reshape(n_p * l_p, d), x_sample.reshape(n_s * l_s, d)], axis=0)
    cond = jnp.zeros((COND_ROWS, d), F32).at[0].set(c_ctx).at[1:1 + n_s].set(c)
    mods = adaln_all(cond, w_ada, b_ada).reshape(depth, COND_ROWS, N_MOD, 1, d)
    nmw = norm_mix_w.reshape(depth, 1, d)
    nfw = norm_ffn_w.reshape(depth, 1, d)
    new_gla, new_ret, new_k, new_v = [], [], [], []
    for layer in range(depth):
        kind, idx = layer % N_MIXERS, layer // N_MIXERS
        h = norm_modulate(x, nmw, layer, mods, 0, lay, BF16)
        if kind == 0:
            x, st = gla_layer(x, h, layer, idx, mods, lay, state_gla, gla_w_in, gla_w_gk1, gla_w_gk2, gla_b_gk,
                              gla_norm_w, gla_w_out)
            new_gla.append(st)
        elif kind == 1:
            x, st = ret_layer(x, h, layer, idx, mods, lay, state_ret, ret_w_in, ret_decay_logit, ret_w_out)
            new_ret.append(st)
        else:
            x, kc, vc = swa_layer(x, h, layer, idx, mods, lay, cache_k, cache_v, swa_w_qkv, swa_b_qkv, swa_sinks,
                                  swa_w_o, swa_b_o)
            new_k.append(kc)
            new_v.append(vc)
        h = norm_modulate(x, nfw, layer, mods, 3, lay, F32)
        x = moe_layer(x, h, layer, mods, lay, moe_w_router, moe_b_router, moe_w_gate_up, moe_b_gate_up,
                      moe_w_down, moe_b_down)
    y = final_norm(x, final_norm_w)
    y_prompt = y[:lay.t_p].reshape(n_p, l_p, d)
    y_sample = y[lay.t_p:].reshape(n_s, l_s, d)
    return (y_prompt, y_sample, jnp.stack(new_gla, axis=1), jnp.stack(new_ret, axis=1),
            jnp.stack(new_k, axis=1), jnp.stack(new_v, axis=1))
```

```python
import functools
from typing import NamedTuple

import numpy as np
import jax
import jax.numpy as jnp
from jax import lax
from jax.experimental import pallas as pl
from jax.experimental.pallas import tpu as pltpu

F32 = jnp.float32
BF16 = jnp.bfloat16

D_MODEL = 4096
DEPTH = 4
GRID_W = 64
N_MIXERS = 3
CHUNK = 64
ROPE_BASE = 10000.0
NORM_EPS = 1e-6
N_MOD = 6
H_GLA, DK_GLA, DV_GLA = 8, 256, 512
GLA_QK = H_GLA * DK_GLA
GLA_V = H_GLA * DV_GLA
GLA_RANK = 16
GLA_GATE_NORM = 16.0
H_RET, DK_RET, DV_RET = 16, 256, 512
RET_QK = H_RET * DK_RET
RET_V = H_RET * DV_RET
N_HEADS_SWA, KV_HEADS, HD_SWA = 64, 8, 64
GROUP = N_HEADS_SWA // KV_HEADS
SWA_Q = N_HEADS_SWA * HD_SWA
SWA_KV = KV_HEADS * HD_SWA
WINDOW = 128
Q_BLOCK = 128
NEG_INF = -1e30
N_EXPERTS = 32
TOP_K = 4
D_FF = 1536
SWIGLU_ALPHA = 1.702
SWIGLU_LIMIT = 7.0

LANES = 128
VMEM_LIMIT = 56 * 1024 * 1024
COND_ROWS = 16
MOE_BM = 256
MOE_RING = 4
MOE_UP_TILE = 512
MOE_DOWN_TILE = 2048
RANK_PAD = 128
SCAN_UNROLL = 4
GATHER_PAD = 4


class Layout(NamedTuple):
    n_p: int
    l_p: int
    n_s: int
    l_s: int

    @property
    def t_p(self):
        return self.n_p * self.l_p

    @property
    def t(self):
        return self.n_p * self.l_p + self.n_s * self.l_s


def _cparams(sem):
    return pltpu.CompilerParams(dimension_semantics=sem, vmem_limit_bytes=VMEM_LIMIT)


def _cond_of_block(lay, bm):
    assert lay.t_p % bm == 0 and lay.l_s % bm == 0
    npb = lay.t_p // bm
    per = lay.l_s // bm
    return lambda i: jnp.where(i < npb, 0, 1 + (i - npb) // per)


def _pick(n, prefs):
    for p in prefs:
        if n % p == 0:
            return p
    return n


def _adaln_kernel(c_ref, w_ref, b_ref, o_ref):
    c = c_ref[...]
    s = (c * jax.nn.sigmoid(c)).astype(BF16)
    o_ref[...] = jnp.dot(s, w_ref[...].astype(BF16), preferred_element_type=F32) + b_ref[...]


def adaln_all(cond, w_ada, b_ada):
    depth, d, n = w_ada.shape
    tn = 512
    return pl.pallas_call(
        _adaln_kernel,
        out_shape=jax.ShapeDtypeStruct((depth, COND_ROWS, n), F32),
        grid=(depth, n // tn),
        in_specs=[
            pl.BlockSpec((COND_ROWS, d), lambda l, j: (0, 0)),
            pl.BlockSpec((None, d, tn), lambda l, j: (l, 0, j)),
            pl.BlockSpec((None, 1, tn), lambda l, j: (l, 0, j)),
        ],
        out_specs=pl.BlockSpec((None, COND_ROWS, tn), lambda l, j: (l, 0, j)),
        compiler_params=_cparams(("arbitrary", "arbitrary")),
        name="adaln",
    )(cond, w_ada, b_ada.reshape(depth, 1, n))


def _normmod_kernel(x_ref, nw_ref, sh_ref, sc_ref, o_ref):
    x = x_ref[...]
    y = x * lax.rsqrt(jnp.mean(x * x, axis=-1, keepdims=True) + NORM_EPS) * nw_ref[...]
    o_ref[...] = (y * (1.0 + sc_ref[...]) + sh_ref[...]).astype(o_ref.dtype)


def norm_modulate(x, norm_w, layer, mods, shift_slot, lay, out_dtype):
    t, d = x.shape
    bm = 256
    cond = _cond_of_block(lay, bm)
    return pl.pallas_call(
        _normmod_kernel,
        out_shape=jax.ShapeDtypeStruct((t, d), out_dtype),
        grid=(t // bm,),
        in_specs=[
            pl.BlockSpec((bm, d), lambda i: (i, 0)),
            pl.BlockSpec((None, 1, d), lambda i: (layer, 0, 0)),
            pl.BlockSpec((None, None, None, 1, d), lambda i: (layer, cond(i), shift_slot, 0, 0)),
            pl.BlockSpec((None, None, None, 1, d), lambda i: (layer, cond(i), shift_slot + 1, 0, 0)),
        ],
        out_specs=pl.BlockSpec((bm, d), lambda i: (i, 0)),
        compiler_params=_cparams(("arbitrary",)),
        name="norm_modulate",
    )(x, norm_w, mods, mods)


def _final_norm_kernel(x_ref, nw_ref, o_ref):
    x = x_ref[...]
    o_ref[...] = x * lax.rsqrt(jnp.mean(x * x, axis=-1, keepdims=True) + NORM_EPS) * nw_ref[...]


def final_norm(x, w):
    t, d = x.shape
    bm = 256
    return pl.pallas_call(
        _final_norm_kernel,
        out_shape=jax.ShapeDtypeStruct((t, d), F32),
        grid=(t // bm,),
        in_specs=[pl.BlockSpec((bm, d), lambda i: (i, 0)), pl.BlockSpec((1, d), lambda i: (0, 0))],
        out_specs=pl.BlockSpec((bm, d), lambda i: (i, 0)),
        compiler_params=_cparams(("arbitrary",)),
        name="final_norm",
    )(x, w.reshape(1, d))


def _mm_kernel(*refs, has_bias, resid):
    it = iter(refs)
    x_ref, w_ref = next(it), next(it)
    b_ref = next(it) if has_bias else None
    res_ref, gate_ref = (next(it), next(it)) if resid else (None, None)
    o_ref, wbf_ref = next(it), next(it)

    @pl.when(pl.program_id(1) == 0)
    def _():
        wbf_ref[...] = w_ref[...].astype(BF16)

    acc = jnp.dot(x_ref[...], wbf_ref[...], preferred_element_type=F32)
    if has_bias:
        acc = acc + b_ref[...]
    if resid:
        acc = res_ref[...] + gate_ref[...] * acc
    o_ref[...] = acc.astype(o_ref.dtype)


def matmul(x, w, idx, *, bias=None, resid=None, out_dtype=F32, lay=None):
    m, k = x.shape
    n = w.shape[-1]
    rows = m if lay is None else np.gcd(lay.t_p, lay.l_s)
    if k * 512 * 4 <= 8 * 1024 * 1024:
        bn, bm = _pick(n, (512, 256, 128)), _pick(rows, (1024, 512, 256, 128))
    else:
        bn, bm = _pick(n, (256, 128)), _pick(rows, (512, 256, 128))
    in_specs = [
        pl.BlockSpec((bm, k), lambda j, i: (i, 0)),
        pl.BlockSpec((None, k, bn), lambda j, i: (idx, 0, j)),
    ]
    args = [x, w]
    if bias is not None:
        in_specs.append(pl.BlockSpec((None, 1, bn), lambda j, i: (idx, 0, j)))
        args.append(bias.reshape(bias.shape[0], 1, n))
    if resid is not None:
        res, mods, layer, slot = resid
        cond = _cond_of_block(lay, bm)
        in_specs.append(pl.BlockSpec((bm, bn), lambda j, i: (i, j)))
        in_specs.append(pl.BlockSpec((None, None, None, 1, bn), lambda j, i: (layer, cond(i), slot, 0, j)))
        args += [res, mods]
    return pl.pallas_call(
        functools.partial(_mm_kernel, has_bias=bias is not None, resid=resid is not None),
        out_shape=jax.ShapeDtypeStruct((m, n), out_dtype),
        grid=(n // bn, m // bm),
        in_specs=in_specs,
        out_specs=pl.BlockSpec((bm, bn), lambda j, i: (i, j)),
        scratch_shapes=[pltpu.VMEM((k, bn), BF16)],
        compiler_params=_cparams(("arbitrary", "arbitrary")),
        name="matmul",
    )(*args)


def _log_sigmoid(z):
    return jnp.minimum(z, 0.0) - jnp.log1p(jnp.exp(-jnp.abs(z)))


def _split2(x):
    hi = x.astype(BF16)
    return hi, (x - hi.astype(F32)).astype(BF16)


def _dot_nt(a, b):
    return lax.dot_general(a, b, (((1,), (1,)), ((), ())), preferred_element_type=F32)


def _dot_tn(a, b):
    return lax.dot_general(a, b, (((0,), (0,)), ((), ())), preferred_element_type=F32)


def _swap_halves(x):
    parts = [pltpu.roll(x[:, o:o + LANES], LANES // 2, 1) for o in range(0, x.shape[1], LANES)]
    return parts[0] if len(parts) == 1 else jnp.concatenate(parts, axis=1)


def _scan_kernel(*refs, seq_len, dk, dv, gla, rope, has_s0, want_state):
    it = iter(refs)
    q_ref, k_ref, v_ref, g_ref = next(it), next(it), next(it), next(it)
    if gla:
        r_ref, w2_ref, bgk_ref, nw_ref = next(it), next(it), next(it), next(it)
    else:
        dl_ref = next(it)
    if rope:
        cos_ref, sin_ref = next(it), next(it)
    s0_ref = next(it) if has_s0 else None
    o_ref = next(it)
    st_ref = next(it) if want_state else None
    st_scr, oacc = next(it), next(it)
    b_scr = next(it) if gla else None

    n_chunks = seq_len // CHUNK
    half = n_chunks // 2
    head = pl.program_id(1)
    row = lax.broadcasted_iota(jnp.int32, (CHUNK, CHUNK), 0)
    col = lax.broadcasted_iota(jnp.int32, (CHUNK, CHUNK), 1)
    scale = dk ** -0.5
    keeps = [(col <= row), (col >= row)]
    if gla:
        tris = [jnp.where(kp, 1.0, 0.0).astype(BF16) for kp in keeps]

        def gate_chunk(c, carry):
            rows = pl.ds(pl.multiple_of(c * CHUNK, CHUNK), CHUNK)
            for d in range(2):
                rr = r_ref[rows, d * RANK_PAD:(d + 1) * RANK_PAD].astype(BF16)
                z = jnp.dot(rr, w2_ref[d].astype(BF16), preferred_element_type=F32) + bgk_ref[d]
                la = _log_sigmoid(z) * (1.0 / GLA_GATE_NORM)
                hi, lo = _split2(la)
                b_scr[d, rows, :] = (jnp.dot(tris[d], hi, preferred_element_type=F32)
                                     + jnp.dot(tris[d], lo, preferred_element_type=F32))
            return carry

        lax.fori_loop(0, n_chunks, gate_chunk, 0, unroll=2)
    else:
        ret_f = []
        for d in range(2):
            la_c = _log_sigmoid(jnp.full((CHUNK, dk), dl_ref[d, head], F32))
            pos = lax.broadcasted_iota(jnp.int32, (CHUNK, dk), 0)
            b_c = (pos + 1 if d == 0 else CHUNK - pos).astype(F32) * la_c
            b_end_c = float(CHUNK) * la_c
            dec_row = jnp.exp(float(CHUNK) * _log_sigmoid(jnp.full((1, dk), dl_ref[d, head], F32)))
            ret_f.append((jnp.exp(b_c), jnp.exp(-b_c) * scale, jnp.exp(b_end_c - b_c) * scale, dec_row))
    for d in range(2):
        if has_s0:
            st_scr[d] = s0_ref[d].T
        else:
            st_scr[d] = jnp.zeros((dv, dk), F32)

    def chunk_out(d, c):
        rows = pl.ds(pl.multiple_of(c * CHUNK, CHUNK), CHUNK)
        qc, kc = q_ref[rows, :], k_ref[rows, :]
        vc = v_ref[rows, :].astype(BF16)
        if rope:
            cs, sn = cos_ref[rows, :], sin_ref[rows, :]
            qc = qc * cs + _swap_halves(qc) * sn
            kc = kc * cs + _swap_halves(kc) * sn
        if gla:
            b = b_scr[d, rows, :]
            last = c * CHUNK + (CHUNK - 1 if d == 0 else 0)
            b_end = b_scr[d, pl.ds(last, 1), :]
            q_dec = (qc * (jnp.exp(b) * scale)).astype(BF16)
            k_inv = (kc * jnp.exp(-b)).astype(BF16)
            k_end = (kc * jnp.exp(b_end - b)).astype(BF16)
            dec = jnp.exp(b_end)
        else:
            e_q, e_inv, e_end, dec = ret_f[d]
            q_dec = (qc * e_q).astype(BF16)
            k_inv = (kc * e_inv).astype(BF16)
            k_end = (kc * e_end).astype(BF16)
        scores = jnp.where(keeps[d], _dot_nt(q_dec, k_inv), 0.0).astype(BF16)
        st = st_scr[d]
        o = jnp.dot(scores, vc, preferred_element_type=F32) + _dot_nt(q_dec, st.astype(BF16))
        st_scr[d] = st * dec + _dot_tn(vc, k_end)
        return rows, o

    def finish(rows, o):
        tot = oacc[rows, :] + o
        y = tot * lax.rsqrt(jnp.mean(tot * tot, axis=-1, keepdims=True) + NORM_EPS)
        if gla:
            y = y * nw_ref[...]
        gg = g_ref[rows, :]
        o_ref[rows, :] = (y * (gg * jax.nn.sigmoid(gg))).astype(o_ref.dtype)

    def first_visit(i, carry):
        rf, of = chunk_out(0, i)
        rb, ob = chunk_out(1, n_chunks - 1 - i)
        oacc[rf, :] = of
        oacc[rb, :] = ob
        return carry

    def second_visit(i, carry):
        rf, of = chunk_out(0, i)
        rb, ob = chunk_out(1, n_chunks - 1 - i)
        finish(rf, of)
        finish(rb, ob)
        return carry

    lax.fori_loop(0, half, first_visit, 0, unroll=SCAN_UNROLL)
    lax.fori_loop(half, n_chunks, second_visit, 0, unroll=SCAN_UNROLL)
    if want_state:
        for d in range(2):
            st_ref[d] = st_scr[d].T


def linear_scan(y, *, seq_len, n_seq, row_block0, n_heads, dk, dv, gla, rope_tabs=None, s0=None, s0_idx=0,
                want_state=False, gla_args=None, decay_logit=None):
    assert seq_len % (2 * CHUNK) == 0
    qk = n_heads * dk
    kb, vb, gb = qk // dk, 2 * qk // dv, (2 * qk + n_heads * dv) // dv
    in_specs = [
        pl.BlockSpec((seq_len, dk), lambda b, h: (row_block0 + b, h)),
        pl.BlockSpec((seq_len, dk), lambda b, h: (row_block0 + b, kb + h)),
        pl.BlockSpec((seq_len, dv), lambda b, h: (row_block0 + b, vb + h)),
        pl.BlockSpec((seq_len, dv), lambda b, h: (row_block0 + b, gb + h)),
    ]
    args = [y, y, y, y]
    if gla:
        r, w2, bgk, nw = gla_args
        in_specs += [
            pl.BlockSpec((seq_len, 2 * RANK_PAD), lambda b, h: (row_block0 + b, 0)),
            pl.BlockSpec((2, RANK_PAD, dk), lambda b, h: (0, 0, h)),
            pl.BlockSpec((2, 1, dk), lambda b, h: (0, 0, h)),
            pl.BlockSpec((1, dv), lambda b, h: (0, 0)),
        ]
        args += [r, w2, bgk, nw]
    else:
        in_specs.append(pl.BlockSpec(memory_space=pltpu.SMEM))
        args.append(decay_logit)
    if rope_tabs is not None:
        in_specs += [pl.BlockSpec((seq_len, dk), lambda b, h: (0, 0))] * 2
        args += list(rope_tabs)
    if s0 is not None:
        in_specs.append(pl.BlockSpec((None, None, 2, None, dk, dv), lambda b, h: (b, s0_idx, 0, h, 0, 0)))
        args.append(s0)
    out_shape = [jax.ShapeDtypeStruct((n_seq * seq_len, n_heads * dv), BF16)]
    out_specs = [pl.BlockSpec((seq_len, dv), lambda b, h: (b, h))]
    if want_state:
        out_shape.append(jax.ShapeDtypeStruct((n_seq, 2, n_heads, dk, dv), F32))
        out_specs.append(pl.BlockSpec((None, 2, None, dk, dv), lambda b, h: (b, 0, h, 0, 0)))
    outs = pl.pallas_call(
        functools.partial(_scan_kernel, seq_len=seq_len, dk=dk, dv=dv, gla=gla, rope=rope_tabs is not None,
                          has_s0=s0 is not None, want_state=want_state),
        out_shape=out_shape,
        grid=(n_seq, n_heads),
        in_specs=in_specs,
        out_specs=out_specs,
        scratch_shapes=[pltpu.VMEM((2, dv, dk), F32), pltpu.VMEM((seq_len, dv), F32)]
        + ([pltpu.VMEM((2, seq_len, dk), F32)] if gla else []),
        compiler_params=_cparams(("arbitrary", "arbitrary")),
        name="linear_scan",
    )(*args)
    return outs[0], (outs[1] if want_state else None)


def _rope_tables(seq_len, d):
    half, nf = d // 2, d // 4
    pos = np.arange(seq_len)
    inv = ROPE_BASE ** (-np.arange(nf, dtype=np.float32) / nf)
    cos_l, sin_l = [], []
    for p in (pos // GRID_W, pos % GRID_W):
        ang = p.astype(np.float32)[:, None] * inv[None, :]
        c, s = np.cos(ang), np.sin(ang)
        cos_l += [c, c]
        sin_l += [-s, s]
    return (jnp.asarray(np.concatenate(cos_l, axis=1), F32), jnp.asarray(np.concatenate(sin_l, axis=1), F32))


def _rope64_kernel(x_ref, cos_ref, s1_ref, s2_ref, o_ref):
    cs, s1, s2 = cos_ref[...], s1_ref[...], s2_ref[...]
    for o in range(0, x_ref.shape[1], LANES):
        x = x_ref[:, o:o + LANES]
        o_ref[:, o:o + LANES] = (x * cs + pltpu.roll(x, LANES - HD_SWA // 4, 1) * s1
                                 + pltpu.roll(x, HD_SWA // 4, 1) * s2)


def rope64(x, seq_len):
    t, w = x.shape
    nf = HD_SWA // 4
    pos = np.arange(seq_len)
    inv = ROPE_BASE ** (-np.arange(nf, dtype=np.float32) / nf)
    ar = (pos // GRID_W).astype(np.float32)[:, None] * inv[None, :]
    ac = (pos % GRID_W).astype(np.float32)[:, None] * inv[None, :]
    zero = np.zeros_like(ar)
    cos = np.concatenate([np.cos(ar), np.cos(ar), np.cos(ac), np.cos(ac)], axis=1)
    s1 = np.concatenate([-np.sin(ar), zero, -np.sin(ac), zero], axis=1)
    s2 = np.concatenate([zero, np.sin(ar), zero, np.sin(ac)], axis=1)
    tabs = [jnp.asarray(np.tile(a, (1, LANES // HD_SWA)), F32) for a in (cos, s1, s2)]
    bm = _pick(seq_len, (256, 128))
    per = seq_len // bm
    return pl.pallas_call(
        _rope64_kernel,
        out_shape=jax.ShapeDtypeStruct((t, w), F32),
        grid=(t // bm,),
        in_specs=[pl.BlockSpec((bm, w), lambda i: (i, 0))] + [pl.BlockSpec((bm, LANES), lambda i: (i % per, 0))] * 3,
        out_specs=pl.BlockSpec((bm, w), lambda i: (i, 0)),
        compiler_params=_cparams(("arbitrary",)),
        name="rope64",
    )(x, *tabs)


def _attn_kernel(*refs, seq_len, q_rows, windowed):
    it = iter(refs)
    q_ref, k_ref, v_ref = next(it), next(it), next(it)
    kc_ref, vc_ref = (next(it), next(it)) if windowed else (None, None)
    sink_ref, o_ref = next(it), next(it)
    rows = GROUP * q_rows
    scale = HD_SWA ** -0.5
    q = q_ref[...].reshape(rows, HD_SWA).astype(BF16)
    sink = sink_ref[...]
    if windowed:
        j = pl.program_id(2)
        span = 3 * Q_BLOCK
        ws = pl.multiple_of(jnp.clip(j * Q_BLOCK - Q_BLOCK, 0, seq_len - span), Q_BLOCK)
        kw = k_ref[pl.ds(ws, span), :].astype(BF16)
        vw = v_ref[pl.ds(ws, span), :].astype(BF16)
        qpos = j * Q_BLOCK + (lax.broadcasted_iota(jnp.int32, (rows, span), 0) & (q_rows - 1))
        kpos = ws + lax.broadcasted_iota(jnp.int32, (rows, span), 1)
        s1 = jnp.where(jnp.abs(qpos - kpos) <= WINDOW, _dot_nt(q, kw) * scale, NEG_INF)
        s2 = _dot_nt(q, kc_ref[...].astype(BF16)) * scale
        m = jnp.maximum(jnp.maximum(jnp.max(s1, axis=-1, keepdims=True), jnp.max(s2, axis=-1, keepdims=True)), sink)
        p1, p2 = jnp.exp(s1 - m), jnp.exp(s2 - m)
        den = jnp.sum(p1, axis=-1, keepdims=True) + jnp.sum(p2, axis=-1, keepdims=True) + jnp.exp(sink - m)
        o = (jnp.dot(p1.astype(BF16), vw, preferred_element_type=F32)
             + jnp.dot(p2.astype(BF16), vc_ref[...].astype(BF16), preferred_element_type=F32))
    else:
        s1 = _dot_nt(q, k_ref[...].astype(BF16)) * scale
        m = jnp.maximum(jnp.max(s1, axis=-1, keepdims=True), sink)
        p1 = jnp.exp(s1 - m)
        den = jnp.sum(p1, axis=-1, keepdims=True) + jnp.exp(sink - m)
        o = jnp.dot(p1.astype(BF16), v_ref[...].astype(BF16), preferred_element_type=F32)
    o_ref[...] = (o / den).reshape(GROUP, q_rows, HD_SWA).astype(o_ref.dtype)


def attention(q, k, v, sinks, *, k_ctx=None, v_ctx=None):
    b, kvh, g, seq_len, hd = q.shape
    windowed = k_ctx is not None
    q_rows = Q_BLOCK if windowed else seq_len
    nq = seq_len // q_rows
    sink_col = jnp.repeat(sinks.astype(F32), q_rows, axis=1).reshape(kvh, g * q_rows, 1)
    in_specs = [
        pl.BlockSpec((None, None, g, q_rows, hd), lambda i, h, j: (i, h, 0, j, 0)),
        pl.BlockSpec((None, None, seq_len, hd), lambda i, h, j: (i, h, 0, 0)),
        pl.BlockSpec((None, None, seq_len, hd), lambda i, h, j: (i, h, 0, 0)),
    ]
    args = [q, k, v]
    if windowed:
        s = k_ctx.shape[2]
        in_specs += [pl.BlockSpec((None, None, s, hd), lambda i, h, j: (i, h, 0, 0))] * 2
        args += [k_ctx, v_ctx]
    in_specs.append(pl.BlockSpec((None, g * q_rows, 1), lambda i, h, j: (h, 0, 0)))
    args.append(sink_col)
    return pl.pallas_call(
        functools.partial(_attn_kernel, seq_len=seq_len, q_rows=q_rows, windowed=windowed),
        out_shape=jax.ShapeDtypeStruct(q.shape, BF16),
        grid=(b, kvh, nq),
        in_specs=in_specs,
        out_specs=pl.BlockSpec((None, None, g, q_rows, hd), lambda i, h, j: (i, h, 0, j, 0)),
        compiler_params=_cparams(("arbitrary", "arbitrary", "arbitrary")),
        name="attention",
    )(*args)


def _router_kernel(x_ref, nw_ref, sh_ref, sc_ref, whi_ref, wlo_ref, b_ref, hrows_ref, idx_ref, gate_ref):
    x = x_ref[...]
    y = x * lax.rsqrt(jnp.mean(x * x, axis=-1, keepdims=True) + NORM_EPS) * nw_ref[...]
    hf = y * (1.0 + sc_ref[...]) + sh_ref[...]
    bm, d = hf.shape
    sub = d // LANES
    for s in range(sub):
        hrows_ref[pl.ds(s, bm, stride=sub), :] = hf[:, s * LANES:(s + 1) * LANES]
    h = hf.astype(BF16)
    logits = (jnp.dot(h, whi_ref[...], preferred_element_type=F32)
              + jnp.dot(h, wlo_ref[...], preferred_element_type=F32) + b_ref[...])
    lane = lax.broadcasted_iota(jnp.int32, logits.shape, 1)
    idx_out = jnp.zeros(logits.shape, jnp.int32)
    val_out = jnp.full(logits.shape, NEG_INF, F32)
    for kk in range(TOP_K):
        m = jnp.max(logits, axis=-1, keepdims=True)
        sel = jnp.min(jnp.where(logits == m, lane, LANES), axis=-1, keepdims=True)
        idx_out = jnp.where(lane == kk, sel, idx_out)
        val_out = jnp.where(lane == kk, m, val_out)
        logits = jnp.where(lane == sel, -jnp.inf, logits)
    e = jnp.exp(val_out - jnp.max(val_out, axis=-1, keepdims=True))
    idx_ref[...] = idx_out
    gate_ref[...] = e / jnp.sum(e, axis=-1, keepdims=True)


def norm_route(x, norm_w, layer, mods, lay, w_r, b_r):
    t, d = x.shape
    sub = d // LANES
    n_exp = w_r.shape[1]
    wp = jnp.zeros((d, LANES), F32).at[:, :n_exp].set(w_r)
    whi = wp.astype(BF16)
    wlo = (wp - whi.astype(F32)).astype(BF16)
    bp = jnp.full((1, LANES), -jnp.inf, F32).at[0, :n_exp].set(b_r)
    bm = 256
    cond = _cond_of_block(lay, bm)
    return pl.pallas_call(
        _router_kernel,
        out_shape=[jax.ShapeDtypeStruct((t * sub, LANES), F32), jax.ShapeDtypeStruct((t, LANES), jnp.int32),
                   jax.ShapeDtypeStruct((t, LANES), F32)],
        grid=(t // bm,),
        in_specs=[pl.BlockSpec((bm, d), lambda i: (i, 0)),
                  pl.BlockSpec((None, 1, d), lambda i: (layer, 0, 0)),
                  pl.BlockSpec((None, None, None, 1, d), lambda i: (layer, cond(i), 3, 0, 0)),
                  pl.BlockSpec((None, None, None, 1, d), lambda i: (layer, cond(i), 4, 0, 0)),
                  pl.BlockSpec((d, LANES), lambda i: (0, 0)),
                  pl.BlockSpec((d, LANES), lambda i: (0, 0)), pl.BlockSpec((1, LANES), lambda i: (0, 0))],
        out_specs=[pl.BlockSpec((bm * sub, LANES), lambda i: (i, 0))] + [pl.BlockSpec((bm, LANES), lambda i: (i, 0))] * 2,
        compiler_params=_cparams(("arbitrary",)),
        name="norm_route",
    )(x, norm_w, mods, mods, whi, wlo, bp)


def _gather_kernel(tok_ref, h_hbm, o_ref, buf, sem, *, sub, pitch):
    i = pl.program_id(0)
    n_steps = pl.num_programs(0)
    bm = o_ref.shape[0]

    def row_copy(step, r, slot):
        tok = tok_ref[step * bm + r]
        return pltpu.make_async_copy(h_hbm.at[pl.ds(pl.multiple_of(tok * sub, sub), sub), :],
                                     buf.at[slot, pl.ds(pl.multiple_of(r * pitch, GATHER_PAD), sub), :], sem.at[slot])

    def start_rows(step, slot):
        for r in range(bm):
            row_copy(step, r, slot).start()

    def finish_rows(step, slot):
        def body(r, c):
            row_copy(step, r, slot).wait()
            return c
        lax.fori_loop(0, bm, body, 0, unroll=8)
        for s in range(sub):
            o_ref[:, s * LANES:(s + 1) * LANES] = buf[slot, pl.ds(s, bm, stride=pitch), :].astype(o_ref.dtype)

    @pl.when(i == 0)
    def _():
        start_rows(0, 0)

    for slot in range(2):
        @pl.when(lax.rem(i, 2) == slot)
        def _(slot=slot):
            @pl.when(i + 1 < n_steps)
            def _():
                start_rows(i + 1, 1 - slot)
            finish_rows(i, slot)


def gather_rows(h_rows, d, slot_tok, bm):
    p = slot_tok.shape[0]
    sub = d // LANES
    pitch = sub + GATHER_PAD
    return pl.pallas_call(
        functools.partial(_gather_kernel, sub=sub, pitch=pitch),
        out_shape=jax.ShapeDtypeStruct((p, d), BF16),
        grid_spec=pltpu.PrefetchScalarGridSpec(
            num_scalar_prefetch=1,
            grid=(p // bm,),
            in_specs=[pl.BlockSpec(memory_space=pl.ANY)],
            out_specs=pl.BlockSpec((bm, d), lambda i, tok: (i, 0)),
            scratch_shapes=[pltpu.VMEM((2, bm * pitch, LANES), F32), pltpu.SemaphoreType.DMA((2,))],
        ),
        compiler_params=_cparams(("arbitrary",)),
        name="moe_gather",
    )(slot_tok, h_rows)


def _expert_rows(n, base, n_act, nb_total, x_hbm, o_hbm, xbuf, obuf, zbuf, xsem, osem, zsem, tn, compute):
    t, e = pl.program_id(0), pl.program_id(1)
    bm = xbuf.shape[1]
    look = MOE_RING - 1
    total = pl.num_programs(0) * n_act
    last_expert = e == pl.num_programs(1) - 1

    def x_copy(cnt):
        rows = pl.ds(pl.multiple_of(lax.rem(cnt, n_act) * bm, bm), bm)
        slot = lax.rem(cnt, MOE_RING)
        return pltpu.make_async_copy(x_hbm.at[rows, :], xbuf.at[slot], xsem.at[slot])

    def o_copy(cnt):
        rows = pl.ds(pl.multiple_of(lax.rem(cnt, n_act) * bm, bm), bm)
        cols = pl.ds(pl.multiple_of(lax.div(cnt, n_act) * tn, tn), tn)
        slot = lax.rem(cnt, 2)
        return pltpu.make_async_copy(obuf.at[slot], o_hbm.at[rows, cols], osem.at[slot])

    @pl.when(jnp.logical_and(t == 0, e == 0))
    def _():
        for c in range(look):
            @pl.when(c < total)
            def _(c=c):
                x_copy(jnp.int32(c)).start()

    cnt0 = t * n_act + base

    def visit(j, carry):
        cnt = cnt0 + j
        x_copy(cnt).wait()

        @pl.when(cnt + look < total)
        def _():
            x_copy(cnt + look).start()

        res = compute(xbuf[lax.rem(cnt, MOE_RING)])

        @pl.when(cnt >= 2)
        def _():
            o_copy(cnt - 2).wait()

        obuf[lax.rem(cnt, 2)] = res
        o_copy(cnt).start()
        return carry

    lax.fori_loop(0, n, visit, 0)

    @pl.when(jnp.logical_and(t == pl.num_programs(0) - 1, last_expert))
    def _():
        for back in (2, 1):
            @pl.when(total >= back)
            def _(back=back):
                o_copy(total - back).wait()

    @pl.when(jnp.logical_and(last_expert, n_act < nb_total))
    def _():
        zbuf[...] = jnp.zeros(zbuf.shape, zbuf.dtype)
        cols = pl.ds(pl.multiple_of(t * tn, tn), tn)

        def tail(blk, carry):
            cp = pltpu.make_async_copy(zbuf, o_hbm.at[pl.ds(pl.multiple_of(blk * bm, bm), bm), cols], zsem)
            cp.start()
            cp.wait()
            return carry

        lax.fori_loop(n_act, nb_total, tail, 0)


def _expert_up_kernel(nblk_ref, bstart_ref, nact_ref, x_hbm, wg_ref, wl_ref, bg_ref, bl_ref, o_hbm,
                      wg_bf, wl_bf, xbuf, obuf, zbuf, xsem, osem, zsem, *, nb_total):
    e = pl.program_id(1)
    n = nblk_ref[e]

    @pl.when(n > 0)
    def _():
        wg_bf[...] = wg_ref[...].astype(BF16)
        wl_bf[...] = wl_ref[...].astype(BF16)

    def compute(x):
        glu = jnp.minimum(jnp.dot(x, wg_bf[...], preferred_element_type=F32) + bg_ref[...], SWIGLU_LIMIT)
        lin = jnp.clip(jnp.dot(x, wl_bf[...], preferred_element_type=F32) + bl_ref[...], -SWIGLU_LIMIT, SWIGLU_LIMIT)
        return (glu * jax.nn.sigmoid(SWIGLU_ALPHA * glu) * (lin + 1.0)).astype(obuf.dtype)

    _expert_rows(n, bstart_ref[e], nact_ref[0], nb_total, x_hbm, o_hbm, xbuf, obuf, zbuf, xsem, osem, zsem,
                 wg_bf.shape[1], compute)


def _expert_down_kernel(nblk_ref, bstart_ref, nact_ref, x_hbm, w_ref, b_ref, o_hbm,
                        w_bf, xbuf, obuf, zbuf, xsem, osem, zsem, *, nb_total):
    e = pl.program_id(1)
    n = nblk_ref[e]

    @pl.when(n > 0)
    def _():
        w_bf[...] = w_ref[...].astype(BF16)

    def compute(x):
        return jnp.dot(x, w_bf[...], preferred_element_type=F32) + b_ref[...]

    _expert_rows(n, bstart_ref[e], nact_ref[0], nb_total, x_hbm, o_hbm, xbuf, obuf, zbuf, xsem, osem, zsem,
                 w_bf.shape[1], compute)


def expert_ffn(xg, sched, layer, w_gu, b_gu, w_dn, b_dn):
    p, d = xg.shape
    nb = p // MOE_BM
    ff = w_dn.shape[2]
    tn, tn2 = MOE_UP_TILE, MOE_DOWN_TILE
    nt = ff // tn
    n_exp = w_gu.shape[1]
    bgu = b_gu.reshape(b_gu.shape[0], n_exp, 1, 2 * ff)
    dma = pltpu.SemaphoreType.DMA
    hidden = pl.pallas_call(
        functools.partial(_expert_up_kernel, nb_total=nb),
        out_shape=jax.ShapeDtypeStruct((p, ff), BF16),
        grid_spec=pltpu.PrefetchScalarGridSpec(
            num_scalar_prefetch=3,
            grid=(nt, n_exp),
            in_specs=[
                pl.BlockSpec(memory_space=pl.ANY),
                pl.BlockSpec((None, None, d, tn), lambda t, e, *_: (layer, e, 0, t)),
                pl.BlockSpec((None, None, d, tn), lambda t, e, *_: (layer, e, 0, nt + t)),
                pl.BlockSpec((None, None, 1, tn), lambda t, e, *_: (layer, e, 0, t)),
                pl.BlockSpec((None, None, 1, tn), lambda t, e, *_: (layer, e, 0, nt + t)),
            ],
            out_specs=pl.BlockSpec(memory_space=pl.ANY),
            scratch_shapes=[pltpu.VMEM((d, tn), BF16), pltpu.VMEM((d, tn), BF16),
                            pltpu.VMEM((MOE_RING, MOE_BM, d), BF16), pltpu.VMEM((2, MOE_BM, tn), BF16),
                            pltpu.VMEM((MOE_BM, tn), BF16), dma((MOE_RING,)), dma((2,)), dma(())],
        ),
        compiler_params=_cparams(("arbitrary", "arbitrary")),
        name="expert_up",
    )(*sched, xg, w_gu, w_gu, bgu, bgu)
    return pl.pallas_call(
        functools.partial(_expert_down_kernel, nb_total=nb),
        out_shape=jax.ShapeDtypeStruct((p, d), F32),
        grid_spec=pltpu.PrefetchScalarGridSpec(
            num_scalar_prefetch=3,
            grid=(d // tn2, n_exp),
            in_specs=[
                pl.BlockSpec(memory_space=pl.ANY),
                pl.BlockSpec((None, None, ff, tn2), lambda t, e, *_: (layer, e, 0, t)),
                pl.BlockSpec((None, None, 1, tn2), lambda t, e, *_: (layer, e, 0, t)),
            ],
            out_specs=pl.BlockSpec(memory_space=pl.ANY),
            scratch_shapes=[pltpu.VMEM((ff, tn2), BF16),
                            pltpu.VMEM((MOE_RING, MOE_BM, ff), BF16), pltpu.VMEM((2, MOE_BM, tn2), F32),
                            pltpu.VMEM((MOE_BM, tn2), F32), dma((MOE_RING,)), dma((2,)), dma(())],
        ),
        compiler_params=_cparams(("arbitrary", "arbitrary")),
        name="expert_down",
    )(*sched, hidden, w_dn, b_dn.reshape(b_dn.shape[0], n_exp, 1, d))


def _combine_kernel(dest_ref, yb_hbm, x_ref, gate_ref, mod_ref, o_ref, buf, sem):
    bt = x_ref.shape[0]

    def row_copy(i):
        return pltpu.make_async_copy(yb_hbm.at[pl.ds(dest_ref[0, 0, i], 1), :], buf.at[pl.ds(i, 1), :], sem)

    def wait(i, c):
        row_copy(i).wait()
        return c

    for i in range(TOP_K * bt):
        row_copy(i).start()
    lax.fori_loop(0, TOP_K * bt, wait, 0, unroll=8)
    g = gate_ref[...]
    y = g[:, 0:1] * buf[0:bt, :]
    for kk in range(1, TOP_K):
        y = y + g[:, kk:kk + 1] * buf[kk * bt:(kk + 1) * bt, :]
    o_ref[...] = x_ref[...] + mod_ref[...] * y


def moe_combine(x, yb, dest, gates, mods, layer, lay):
    t, d = x.shape
    bt = 128
    cond = _cond_of_block(lay, bt)
    dest_blk = dest.reshape(t // bt, bt, TOP_K).transpose(0, 2, 1).reshape(t // bt, 1, TOP_K * bt)
    return pl.pallas_call(
        _combine_kernel,
        out_shape=jax.ShapeDtypeStruct((t, d), F32),
        grid=(t // bt,),
        in_specs=[pl.BlockSpec((1, 1, TOP_K * bt), lambda i: (i, 0, 0), memory_space=pltpu.SMEM),
                  pl.BlockSpec(memory_space=pl.ANY),
                  pl.BlockSpec((bt, d), lambda i: (i, 0)),
                  pl.BlockSpec((bt, LANES), lambda i: (i, 0)),
                  pl.BlockSpec((None, None, None, 1, d), lambda i: (layer, cond(i), 5, 0, 0))],
        out_specs=pl.BlockSpec((bt, d), lambda i: (i, 0)),
        scratch_shapes=[pltpu.VMEM((TOP_K * bt, d), F32), pltpu.SemaphoreType.DMA(())],
        compiler_params=_cparams(("arbitrary",)),
        name="moe_combine",
    )(dest_blk, yb, x, gates, mods)


def _moe_schedule(top_i, n_exp):
    t = top_i.shape[0]
    n_asg = t * TOP_K
    flat_e = top_i.reshape(n_asg)
    onehot = (flat_e[:, None] == jnp.arange(n_exp, dtype=jnp.int32)[None, :]).astype(jnp.int32)
    oh = onehot.astype(F32).reshape(n_asg // LANES, LANES, n_exp)
    within = jnp.einsum("ij,bjk->bik", jnp.tril(jnp.ones((LANES, LANES), F32)), oh)
    totals = within[:, -1, :]
    csum = (within + (jnp.cumsum(totals, axis=0) - totals)[:, None, :]).reshape(n_asg, n_exp).astype(jnp.int32)
    counts = csum[-1]
    pcounts = (counts + MOE_BM - 1) // MOE_BM * MOE_BM
    pends = jnp.cumsum(pcounts)
    pstarts = pends - pcounts
    dest = jnp.sum(onehot * (csum - 1 + pstarts[None, :]), axis=1)
    nb = -(-(n_asg + n_exp * (MOE_BM - 1)) // MOE_BM)
    slot_tok = jnp.zeros((nb * MOE_BM,), jnp.int32).at[dest].set(jnp.arange(n_asg, dtype=jnp.int32) // TOP_K)
    nblk = (pcounts // MOE_BM).astype(jnp.int32)
    bstart = (pstarts // MOE_BM).astype(jnp.int32)
    n_act = (pends[-1:] // MOE_BM).astype(jnp.int32)
    return dest.reshape(t, TOP_K), slot_tok, (nblk, bstart, n_act)


def moe_layer(x, norm_w, layer, mods, lay, w_r, b_r, w_gu, b_gu, w_dn, b_dn):
    h_rows, top_i, gates = norm_route(x, norm_w, layer, mods, lay, w_r[layer], b_r[layer])
    dest, slot_tok, sched = _moe_schedule(top_i[:, :TOP_K], w_gu.shape[1])
    xg = gather_rows(h_rows, x.shape[1], slot_tok, MOE_BM)
    yb = expert_ffn(xg, sched, layer, w_gu, b_gu, w_dn, b_dn)
    return moe_combine(x, yb, dest, gates, mods, layer, lay)


def gla_layer(x, h, layer, idx, mods, lay, state_gla, w_in, w_gk1, w_gk2, b_gk, norm_w, w_out):
    d = h.shape[1]
    y = matmul(h, w_in, idx)
    w1 = jnp.zeros((1, d, 2 * RANK_PAD), F32)
    w1 = w1.at[0, :, :GLA_RANK].set(w_gk1[idx, 0]).at[0, :, RANK_PAD:RANK_PAD + GLA_RANK].set(w_gk1[idx, 1])
    r = matmul(h, w1, 0)
    w2 = jnp.zeros((2, RANK_PAD, GLA_QK), F32).at[:, :GLA_RANK, :].set(w_gk2[idx])
    gla_args = (r, w2, b_gk[idx].reshape(2, 1, GLA_QK), norm_w[idx].reshape(1, DV_GLA))
    common = dict(n_heads=H_GLA, dk=DK_GLA, dv=DV_GLA, gla=True, gla_args=gla_args)
    o_p, st = linear_scan(y, seq_len=lay.l_p, n_seq=lay.n_p, row_block0=0, want_state=True, **common)
    o_s, _ = linear_scan(y, seq_len=lay.l_s, n_seq=lay.n_s, row_block0=lay.t_p // lay.l_s,
                         s0=state_gla, s0_idx=idx, **common)
    o = jnp.concatenate([o_p, o_s], axis=0)
    return matmul(o, w_out, idx, resid=(x, mods, layer, 2), lay=lay), st


def ret_layer(x, h, layer, idx, mods, lay, state_ret, w_in, decay_logit, w_out):
    y = matmul(h, w_in, idx)
    common = dict(n_heads=H_RET, dk=DK_RET, dv=DV_RET, gla=False, decay_logit=decay_logit[idx])
    o_p, st = linear_scan(y, seq_len=lay.l_p, n_seq=lay.n_p, row_block0=0, want_state=True, **common)
    o_s, _ = linear_scan(y, seq_len=lay.l_s, n_seq=lay.n_s, row_block0=lay.t_p // lay.l_s,
                         rope_tabs=_rope_tables(lay.l_s, DK_RET), s0=state_ret, s0_idx=idx, **common)
    o = jnp.concatenate([o_p, o_s], axis=0)
    return matmul(o, w_out, idx, resid=(x, mods, layer, 2), lay=lay), st


def swa_layer(x, h, layer, idx, mods, lay, cache_k, cache_v, w_qkv, b_qkv, sinks, w_o, b_o):
    qkv = matmul(h, w_qkv, idx, bias=b_qkv)
    sk = sinks[idx].reshape(KV_HEADS, GROUP)

    def heads(a, n_seq, seq_len, nh):
        return a.reshape(n_seq, seq_len, nh, HD_SWA).transpose(0, 2, 1, 3)

    qkv_p = qkv[:lay.t_p]
    k_p, v_p = qkv_p[:, SWA_Q:SWA_Q + SWA_KV], qkv_p[:, SWA_Q + SWA_KV:]
    q_h = heads(qkv_p[:, :SWA_Q], lay.n_p, lay.l_p, N_HEADS_SWA).reshape(lay.n_p, KV_HEADS, GROUP, lay.l_p, HD_SWA)
    o_p = attention(q_h, heads(k_p, lay.n_p, lay.l_p, KV_HEADS), heads(v_p, lay.n_p, lay.l_p, KV_HEADS), sk)
    o_p = o_p.reshape(lay.n_p, N_HEADS_SWA, lay.l_p, HD_SWA).transpose(0, 2, 1, 3).reshape(lay.t_p, SWA_Q)

    qkv_s = qkv[lay.t_p:]
    qk_rot = rope64(qkv_s[:, :SWA_Q + SWA_KV], lay.l_s)
    q_h = heads(qk_rot[:, :SWA_Q], lay.n_s, lay.l_s, N_HEADS_SWA).reshape(lay.n_s, KV_HEADS, GROUP, lay.l_s, HD_SWA)
    k_h = heads(qk_rot[:, SWA_Q:], lay.n_s, lay.l_s, KV_HEADS)
    v_h = heads(qkv_s[:, SWA_Q + SWA_KV:], lay.n_s, lay.l_s, KV_HEADS)
    kc = cache_k[:, idx].transpose(0, 2, 1, 3)
    vc = cache_v[:, idx].transpose(0, 2, 1, 3)
    o_s = attention(q_h, k_h, v_h, sk, k_ctx=kc, v_ctx=vc)
    o_s = o_s.reshape(lay.n_s, N_HEADS_SWA, lay.l_s, HD_SWA).transpose(0, 2, 1, 3).reshape(lay.n_s * lay.l_s, SWA_Q)

    o = jnp.concatenate([o_p, o_s], axis=0)
    x_new = matmul(o, w_o, idx, bias=b_o, resid=(x, mods, layer, 2), lay=lay)
    new_k = k_p.reshape(lay.n_p, lay.l_p, KV_HEADS, HD_SWA)
    new_v = v_p.reshape(lay.n_p, lay.l_p, KV_HEADS, HD_SWA)
    return x_new, new_k, new_v


def kernel(x_prompt, x_sample, state_gla, state_ret, cache_k, cache_v, c, c_ctx, norm_mix_w, norm_ffn_w, w_ada, b_ada, gla_w_in, gla_w_gk1, gla_w_gk2, gla_b_gk, gla_norm_w, gla_w_out, ret_w_in, ret_decay_logit, ret_w_out, swa_w_qkv, swa_b_qkv, swa_sinks, swa_w_o, swa_b_o, moe_w_router, moe_b_router, moe_w_gate_up, moe_b_gate_up, moe_w_down, moe_b_down, final_norm_w):
    n_p, l_p, d = x_prompt.shape
    n_s, l_s, _ = x_sample.shape
    lay = Layout(n_p, l_p, n_s, l_s)
    depth = w_ada.shape[0]
    x = jnp.concatenate([x_prompt.reshape(n_p * l_p, d), x_sample.reshape(n_s * l_s, d)], axis=0)
    cond = jnp.zeros((COND_ROWS, d), F32).at[0].set(c_ctx).at[1:1 + n_s].set(c)
    mods = adaln_all(cond, w_ada, b_ada).reshape(depth, COND_ROWS, N_MOD, 1, d)
    nmw = norm_mix_w.reshape(depth, 1, d)
    nfw = norm_ffn_w.reshape(depth, 1, d)
    new_gla, new_ret, new_k, new_v = [], [], [], []
    for layer in range(depth):
        kind, idx = layer % N_MIXERS, layer // N_MIXERS
        h = norm_modulate(x, nmw, layer, mods, 0, lay, BF16)
        if kind == 0:
            x, st = gla_layer(x, h, layer, idx, mods, lay, state_gla, gla_w_in, gla_w_gk1, gla_w_gk2, gla_b_gk,
                              gla_norm_w, gla_w_out)
            new_gla.append(st)
        elif kind == 1:
            x, st = ret_layer(x, h, layer, idx, mods, lay, state_ret, ret_w_in, ret_decay_logit, ret_w_out)
            new_ret.append(st)
        else:
            x, kc, vc = swa_layer(x, h, layer, idx, mods, lay, cache_k, cache_v, swa_w_qkv, swa_b_qkv, swa_sinks,
                                  swa_w_o, swa_b_o)
            new_k.append(kc)
            new_v.append(vc)
        x = moe_layer(x, nfw, layer, mods, lay, moe_w_router, moe_b_router, moe_w_gate_up, moe_b_gate_up,
                      moe_w_down, moe_b_down)
    y = final_norm(x, final_norm_w)
    y_prompt = y[:lay.t_p].reshape(n_p, l_p, d)
    y_sample = y[lay.t_p:].reshape(n_s, l_s, d)
    return (y_prompt, y_sample, jnp.stack(new_gla, axis=1), jnp.stack(new_ret, axis=1),
            jnp.stack(new_k, axis=1), jnp.stack(new_v, axis=1))
```

```python
import functools
from typing import NamedTuple

import numpy as np
import jax
import jax.numpy as jnp
from jax import lax
from jax.experimental import pallas as pl
from jax.experimental.pallas import tpu as pltpu

F32 = jnp.float32
BF16 = jnp.bfloat16

D_MODEL = 4096
DEPTH = 4
GRID_W = 64
N_MIXERS = 3
CHUNK = 64
ROPE_BASE = 10000.0
NORM_EPS = 1e-6
N_MOD = 6
H_GLA, DK_GLA, DV_GLA = 8, 256, 512
GLA_QK = H_GLA * DK_GLA
GLA_V = H_GLA * DV_GLA
GLA_RANK = 16
GLA_GATE_NORM = 16.0
H_RET, DK_RET, DV_RET = 16, 256, 512
RET_QK = H_RET * DK_RET
RET_V = H_RET * DV_RET
N_HEADS_SWA, KV_HEADS, HD_SWA = 64, 8, 64
GROUP = N_HEADS_SWA // KV_HEADS
SWA_Q = N_HEADS_SWA * HD_SWA
SWA_KV = KV_HEADS * HD_SWA
WINDOW = 128
Q_BLOCK = 128
NEG_INF = -1e30
N_EXPERTS = 32
TOP_K = 4
D_FF = 1536
SWIGLU_ALPHA = 1.702
SWIGLU_LIMIT = 7.0

LANES = 128
VMEM_LIMIT = 56 * 1024 * 1024
COND_ROWS = 16
MOE_BM = 256
MOE_RING = 4
MOE_UP_TILE = 512
MOE_DOWN_TILE = 2048
RANK_PAD = 128
SCAN_UNROLL = 4
GATHER_PAD = 4


class Layout(NamedTuple):
    n_p: int
    l_p: int
    n_s: int
    l_s: int

    @property
    def t_p(self):
        return self.n_p * self.l_p

    @property
    def t(self):
        return self.n_p * self.l_p + self.n_s * self.l_s


def _cparams(sem):
    return pltpu.CompilerParams(dimension_semantics=sem, vmem_limit_bytes=VMEM_LIMIT)


def _cond_of_block(lay, bm):
    assert lay.t_p % bm == 0 and lay.l_s % bm == 0
    npb = lay.t_p // bm
    per = lay.l_s // bm
    return lambda i: jnp.where(i < npb, 0, 1 + (i - npb) // per)


def _pick(n, prefs):
    for p in prefs:
        if n % p == 0:
            return p
    return n


def _adaln_kernel(c_ref, w_ref, b_ref, o_ref):
    c = c_ref[...]
    s = (c * jax.nn.sigmoid(c)).astype(BF16)
    o_ref[...] = jnp.dot(s, w_ref[...].astype(BF16), preferred_element_type=F32) + b_ref[...]


def adaln_all(cond, w_ada, b_ada):
    depth, d, n = w_ada.shape
    tn = 512
    return pl.pallas_call(
        _adaln_kernel,
        out_shape=jax.ShapeDtypeStruct((depth, COND_ROWS, n), F32),
        grid=(depth, n // tn),
        in_specs=[
            pl.BlockSpec((COND_ROWS, d), lambda l, j: (0, 0)),
            pl.BlockSpec((None, d, tn), lambda l, j: (l, 0, j)),
            pl.BlockSpec((None, 1, tn), lambda l, j: (l, 0, j)),
        ],
        out_specs=pl.BlockSpec((None, COND_ROWS, tn), lambda l, j: (l, 0, j)),
        compiler_params=_cparams(("arbitrary", "arbitrary")),
        name="adaln",
    )(cond, w_ada, b_ada.reshape(depth, 1, n))


def _normmod_kernel(x_ref, nw_ref, sh_ref, sc_ref, o_ref):
    x = x_ref[...]
    y = x * lax.rsqrt(jnp.mean(x * x, axis=-1, keepdims=True) + NORM_EPS) * nw_ref[...]
    o_ref[...] = (y * (1.0 + sc_ref[...]) + sh_ref[...]).astype(o_ref.dtype)


def norm_modulate(x, norm_w, layer, mods, shift_slot, lay, out_dtype):
    t, d = x.shape
    bm = 256
    cond = _cond_of_block(lay, bm)
    return pl.pallas_call(
        _normmod_kernel,
        out_shape=jax.ShapeDtypeStruct((t, d), out_dtype),
        grid=(t // bm,),
        in_specs=[
            pl.BlockSpec((bm, d), lambda i: (i, 0)),
            pl.BlockSpec((None, 1, d), lambda i: (layer, 0, 0)),
            pl.BlockSpec((None, None, None, 1, d), lambda i: (layer, cond(i), shift_slot, 0, 0)),
            pl.BlockSpec((None, None, None, 1, d), lambda i: (layer, cond(i), shift_slot + 1, 0, 0)),
        ],
        out_specs=pl.BlockSpec((bm, d), lambda i: (i, 0)),
        compiler_params=_cparams(("arbitrary",)),
        name="norm_modulate",
    )(x, norm_w, mods, mods)


def _mm_kernel(*refs, has_bias, resid):
    it = iter(refs)
    x_ref, w_ref = next(it), next(it)
    b_ref = next(it) if has_bias else None
    res_ref, gate_ref = (next(it), next(it)) if resid else (None, None)
    o_ref, wbf_ref = next(it), next(it)

    @pl.when(pl.program_id(1) == 0)
    def _():
        wbf_ref[...] = w_ref[...].astype(BF16)

    acc = jnp.dot(x_ref[...], wbf_ref[...], preferred_element_type=F32)
    if has_bias:
        acc = acc + b_ref[...]
    if resid:
        acc = res_ref[...] + gate_ref[...] * acc
    o_ref[...] = acc.astype(o_ref.dtype)


def matmul(x, w, idx, *, bias=None, resid=None, out_dtype=F32, lay=None):
    m, k = x.shape
    n = w.shape[-1]
    rows = m if lay is None else np.gcd(lay.t_p, lay.l_s)
    if k * 512 * 4 <= 8 * 1024 * 1024:
        bn, bm = _pick(n, (512, 256, 128)), _pick(rows, (1024, 512, 256, 128))
    else:
        bn, bm = _pick(n, (256, 128)), _pick(rows, (512, 256, 128))
    in_specs = [
        pl.BlockSpec((bm, k), lambda j, i: (i, 0)),
        pl.BlockSpec((None, k, bn), lambda j, i: (idx, 0, j)),
    ]
    args = [x, w]
    if bias is not None:
        in_specs.append(pl.BlockSpec((None, 1, bn), lambda j, i: (idx, 0, j)))
        args.append(bias.reshape(bias.shape[0], 1, n))
    if resid is not None:
        res, mods, layer, slot = resid
        cond = _cond_of_block(lay, bm)
        in_specs.append(pl.BlockSpec((bm, bn), lambda j, i: (i, j)))
        in_specs.append(pl.BlockSpec((None, None, None, 1, bn), lambda j, i: (layer, cond(i), slot, 0, j)))
        args += [res, mods]
    return pl.pallas_call(
        functools.partial(_mm_kernel, has_bias=bias is not None, resid=resid is not None),
        out_shape=jax.ShapeDtypeStruct((m, n), out_dtype),
        grid=(n // bn, m // bm),
        in_specs=in_specs,
        out_specs=pl.BlockSpec((bm, bn), lambda j, i: (i, j)),
        scratch_shapes=[pltpu.VMEM((k, bn), BF16)],
        compiler_params=_cparams(("arbitrary", "arbitrary")),
        name="matmul",
    )(*args)


def _log_sigmoid(z):
    return jnp.minimum(z, 0.0) - jnp.log1p(jnp.exp(-jnp.abs(z)))


def _split2(x):
    hi = x.astype(BF16)
    return hi, (x - hi.astype(F32)).astype(BF16)


def _dot_nt(a, b):
    return lax.dot_general(a, b, (((1,), (1,)), ((), ())), preferred_element_type=F32)


def _dot_tn(a, b):
    return lax.dot_general(a, b, (((0,), (0,)), ((), ())), preferred_element_type=F32)


def _swap_halves(x):
    parts = [pltpu.roll(x[:, o:o + LANES], LANES // 2, 1) for o in range(0, x.shape[1], LANES)]
    return parts[0] if len(parts) == 1 else jnp.concatenate(parts, axis=1)


def _scan_kernel(*refs, seq_len, dk, dv, gla, rope, has_s0, want_state):
    it = iter(refs)
    q_ref, k_ref, v_ref, g_ref = next(it), next(it), next(it), next(it)
    if gla:
        r_ref, w2_ref, bgk_ref, nw_ref = next(it), next(it), next(it), next(it)
    else:
        dl_ref = next(it)
    if rope:
        cos_ref, sin_ref = next(it), next(it)
    s0_ref = next(it) if has_s0 else None
    o_ref = next(it)
    st_ref = next(it) if want_state else None
    st_scr, oacc = next(it), next(it)
    b_scr = next(it) if gla else None

    n_chunks = seq_len // CHUNK
    half = n_chunks // 2
    head = pl.program_id(1)
    row = lax.broadcasted_iota(jnp.int32, (CHUNK, CHUNK), 0)
    col = lax.broadcasted_iota(jnp.int32, (CHUNK, CHUNK), 1)
    scale = dk ** -0.5
    keeps = [(col <= row), (col >= row)]
    if gla:
        tris = [jnp.where(kp, 1.0, 0.0).astype(BF16) for kp in keeps]

        def gate_chunk(c, carry):
            rows = pl.ds(pl.multiple_of(c * CHUNK, CHUNK), CHUNK)
            for d in range(2):
                rr = r_ref[rows, d * RANK_PAD:(d + 1) * RANK_PAD].astype(BF16)
                z = jnp.dot(rr, w2_ref[d].astype(BF16), preferred_element_type=F32) + bgk_ref[d]
                la = _log_sigmoid(z) * (1.0 / GLA_GATE_NORM)
                hi, lo = _split2(la)
                b_scr[d, rows, :] = (jnp.dot(tris[d], hi, preferred_element_type=F32)
                                     + jnp.dot(tris[d], lo, preferred_element_type=F32))
            return carry

        lax.fori_loop(0, n_chunks, gate_chunk, 0, unroll=2)
    else:
        ret_f = []
        for d in range(2):
            la_c = _log_sigmoid(jnp.full((CHUNK, dk), dl_ref[d, head], F32))
            pos = lax.broadcasted_iota(jnp.int32, (CHUNK, dk), 0)
            b_c = (pos + 1 if d == 0 else CHUNK - pos).astype(F32) * la_c
            b_end_c = float(CHUNK) * la_c
            dec_row = jnp.exp(float(CHUNK) * _log_sigmoid(jnp.full((1, dk), dl_ref[d, head], F32)))
            ret_f.append((jnp.exp(b_c), jnp.exp(-b_c) * scale, jnp.exp(b_end_c - b_c) * scale, dec_row))
    for d in range(2):
        if has_s0:
            st_scr[d] = s0_ref[d].T
        else:
            st_scr[d] = jnp.zeros((dv, dk), F32)

    def chunk_out(d, c):
        rows = pl.ds(pl.multiple_of(c * CHUNK, CHUNK), CHUNK)
        qc, kc = q_ref[rows, :], k_ref[rows, :]
        vc = v_ref[rows, :].astype(BF16)
        if rope:
            cs, sn = cos_ref[rows, :], sin_ref[rows, :]
            qc = qc * cs + _swap_halves(qc) * sn
            kc = kc * cs + _swap_halves(kc) * sn
        if gla:
            b = b_scr[d, rows, :]
            last = c * CHUNK + (CHUNK - 1 if d == 0 else 0)
            b_end = b_scr[d, pl.ds(last, 1), :]
            q_dec = (qc * (jnp.exp(b) * scale)).astype(BF16)
            k_inv = (kc * jnp.exp(-b)).astype(BF16)
            k_end = (kc * jnp.exp(b_end - b)).astype(BF16)
            dec = jnp.exp(b_end)
        else:
            e_q, e_inv, e_end, dec = ret_f[d]
            q_dec = (qc * e_q).astype(BF16)
            k_inv = (kc * e_inv).astype(BF16)
            k_end = (kc * e_end).astype(BF16)
        scores = jnp.where(keeps[d], _dot_nt(q_dec, k_inv), 0.0).astype(BF16)
        st = st_scr[d]
        o = jnp.dot(scores, vc, preferred_element_type=F32) + _dot_nt(q_dec, st.astype(BF16))
        st_scr[d] = st * dec + _dot_tn(vc, k_end)
        return rows, o

    def finish(rows, o):
        tot = oacc[rows, :] + o
        y = tot * lax.rsqrt(jnp.mean(tot * tot, axis=-1, keepdims=True) + NORM_EPS)
        if gla:
            y = y * nw_ref[...]
        gg = g_ref[rows, :]
        o_ref[rows, :] = (y * (gg * jax.nn.sigmoid(gg))).astype(o_ref.dtype)

    def first_visit(i, carry):
        rf, of = chunk_out(0, i)
        rb, ob = chunk_out(1, n_chunks - 1 - i)
        oacc[rf, :] = of
        oacc[rb, :] = ob
        return carry

    def second_visit(i, carry):
        rf, of = chunk_out(0, i)
        rb, ob = chunk_out(1, n_chunks - 1 - i)
        finish(rf, of)
        finish(rb, ob)
        return carry

    lax.fori_loop(0, half, first_visit, 0, unroll=SCAN_UNROLL)
    lax.fori_loop(half, n_chunks, second_visit, 0, unroll=SCAN_UNROLL)
    if want_state:
        for d in range(2):
            st_ref[d] = st_scr[d].T


def linear_scan(y, *, seq_len, n_seq, row_block0, n_heads, dk, dv, gla, rope_tabs=None, s0=None, s0_idx=0,
                want_state=False, gla_args=None, decay_logit=None):
    assert seq_len % (2 * CHUNK) == 0
    qk = n_heads * dk
    kb, vb, gb = qk // dk, 2 * qk // dv, (2 * qk + n_heads * dv) // dv
    in_specs = [
        pl.BlockSpec((seq_len, dk), lambda b, h: (row_block0 + b, h)),
        pl.BlockSpec((seq_len, dk), lambda b, h: (row_block0 + b, kb + h)),
        pl.BlockSpec((seq_len, dv), lambda b, h: (row_block0 + b, vb + h)),
        pl.BlockSpec((seq_len, dv), lambda b, h: (row_block0 + b, gb + h)),
    ]
    args = [y, y, y, y]
    if gla:
        r, w2, bgk, nw = gla_args
        in_specs += [
            pl.BlockSpec((seq_len, 2 * RANK_PAD), lambda b, h: (row_block0 + b, 0)),
            pl.BlockSpec((2, RANK_PAD, dk), lambda b, h: (0, 0, h)),
            pl.BlockSpec((2, 1, dk), lambda b, h: (0, 0, h)),
            pl.BlockSpec((1, dv), lambda b, h: (0, 0)),
        ]
        args += [r, w2, bgk, nw]
    else:
        in_specs.append(pl.BlockSpec(memory_space=pltpu.SMEM))
        args.append(decay_logit)
    if rope_tabs is not None:
        in_specs += [pl.BlockSpec((seq_len, dk), lambda b, h: (0, 0))] * 2
        args += list(rope_tabs)
    if s0 is not None:
        in_specs.append(pl.BlockSpec((None, None, 2, None, dk, dv), lambda b, h: (b, s0_idx, 0, h, 0, 0)))
        args.append(s0)
    out_shape = [jax.ShapeDtypeStruct((n_seq * seq_len, n_heads * dv), BF16)]
    out_specs = [pl.BlockSpec((seq_len, dv), lambda b, h: (b, h))]
    if want_state:
        out_shape.append(jax.ShapeDtypeStruct((n_seq, 2, n_heads, dk, dv), F32))
        out_specs.append(pl.BlockSpec((None, 2, None, dk, dv), lambda b, h: (b, 0, h, 0, 0)))
    outs = pl.pallas_call(
        functools.partial(_scan_kernel, seq_len=seq_len, dk=dk, dv=dv, gla=gla, rope=rope_tabs is not None,
                          has_s0=s0 is not None, want_state=want_state),
        out_shape=out_shape,
        grid=(n_seq, n_heads),
        in_specs=in_specs,
        out_specs=out_specs,
        scratch_shapes=[pltpu.VMEM((2, dv, dk), F32), pltpu.VMEM((seq_len, dv), F32)]
        + ([pltpu.VMEM((2, seq_len, dk), F32)] if gla else []),
        compiler_params=_cparams(("arbitrary", "arbitrary")),
        name="linear_scan",
    )(*args)
    return outs[0], (outs[1] if want_state else None)


def _rope_tables(seq_len, d):
    half, nf = d // 2, d // 4
    pos = np.arange(seq_len)
    inv = ROPE_BASE ** (-np.arange(nf, dtype=np.float32) / nf)
    cos_l, sin_l = [], []
    for p in (pos // GRID_W, pos % GRID_W):
        ang = p.astype(np.float32)[:, None] * inv[None, :]
        c, s = np.cos(ang), np.sin(ang)
        cos_l += [c, c]
        sin_l += [-s, s]
    return (jnp.asarray(np.concatenate(cos_l, axis=1), F32), jnp.asarray(np.concatenate(sin_l, axis=1), F32))


def _rope64_kernel(x_ref, cos_ref, s1_ref, s2_ref, o_ref):
    cs, s1, s2 = cos_ref[...], s1_ref[...], s2_ref[...]
    for o in range(0, x_ref.shape[1], LANES):
        x = x_ref[:, o:o + LANES]
        o_ref[:, o:o + LANES] = (x * cs + pltpu.roll(x, LANES - HD_SWA // 4, 1) * s1
                                 + pltpu.roll(x, HD_SWA // 4, 1) * s2)


def rope64(x, seq_len):
    t, w = x.shape
    nf = HD_SWA // 4
    pos = np.arange(seq_len)
    inv = ROPE_BASE ** (-np.arange(nf, dtype=np.float32) / nf)
    ar = (pos // GRID_W).astype(np.float32)[:, None] * inv[None, :]
    ac = (pos % GRID_W).astype(np.float32)[:, None] * inv[None, :]
    zero = np.zeros_like(ar)
    cos = np.concatenate([np.cos(ar), np.cos(ar), np.cos(ac), np.cos(ac)], axis=1)
    s1 = np.concatenate([-np.sin(ar), zero, -np.sin(ac), zero], axis=1)
    s2 = np.concatenate([zero, np.sin(ar), zero, np.sin(ac)], axis=1)
    tabs = [jnp.asarray(np.tile(a, (1, LANES // HD_SWA)), F32) for a in (cos, s1, s2)]
    bm = _pick(seq_len, (256, 128))
    per = seq_len // bm
    return pl.pallas_call(
        _rope64_kernel,
        out_shape=jax.ShapeDtypeStruct((t, w), F32),
        grid=(t // bm,),
        in_specs=[pl.BlockSpec((bm, w), lambda i: (i, 0))] + [pl.BlockSpec((bm, LANES), lambda i: (i % per, 0))] * 3,
        out_specs=pl.BlockSpec((bm, w), lambda i: (i, 0)),
        compiler_params=_cparams(("arbitrary",)),
        name="rope64",
    )(x, *tabs)


def _attn_kernel(*refs, seq_len, q_rows, windowed):
    it = iter(refs)
    q_ref, k_ref, v_ref = next(it), next(it), next(it)
    kc_ref, vc_ref = (next(it), next(it)) if windowed else (None, None)
    sink_ref, o_ref = next(it), next(it)
    rows = GROUP * q_rows
    scale = HD_SWA ** -0.5
    q = q_ref[...].reshape(rows, HD_SWA).astype(BF16)
    sink = sink_ref[...]
    if windowed:
        j = pl.program_id(2)
        span = 3 * Q_BLOCK
        ws = pl.multiple_of(jnp.clip(j * Q_BLOCK - Q_BLOCK, 0, seq_len - span), Q_BLOCK)
        kw = k_ref[pl.ds(ws, span), :].astype(BF16)
        vw = v_ref[pl.ds(ws, span), :].astype(BF16)
        qpos = j * Q_BLOCK + (lax.broadcasted_iota(jnp.int32, (rows, span), 0) & (q_rows - 1))
        kpos = ws + lax.broadcasted_iota(jnp.int32, (rows, span), 1)
        s1 = jnp.where(jnp.abs(qpos - kpos) <= WINDOW, _dot_nt(q, kw) * scale, NEG_INF)
        s2 = _dot_nt(q, kc_ref[...].astype(BF16)) * scale
        m = jnp.maximum(jnp.maximum(jnp.max(s1, axis=-1, keepdims=True), jnp.max(s2, axis=-1, keepdims=True)), sink)
        p1, p2 = jnp.exp(s1 - m), jnp.exp(s2 - m)
        den = jnp.sum(p1, axis=-1, keepdims=True) + jnp.sum(p2, axis=-1, keepdims=True) + jnp.exp(sink - m)
        o = (jnp.dot(p1.astype(BF16), vw, preferred_element_type=F32)
             + jnp.dot(p2.astype(BF16), vc_ref[...].astype(BF16), preferred_element_type=F32))
    else:
        s1 = _dot_nt(q, k_ref[...].astype(BF16)) * scale
        m = jnp.maximum(jnp.max(s1, axis=-1, keepdims=True), sink)
        p1 = jnp.exp(s1 - m)
        den = jnp.sum(p1, axis=-1, keepdims=True) + jnp.exp(sink - m)
        o = jnp.dot(p1.astype(BF16), v_ref[...].astype(BF16), preferred_element_type=F32)
    o_ref[...] = (o / den).reshape(GROUP, q_rows, HD_SWA).astype(o_ref.dtype)


def attention(q, k, v, sinks, *, k_ctx=None, v_ctx=None):
    b, kvh, g, seq_len, hd = q.shape
    windowed = k_ctx is not None
    q_rows = Q_BLOCK if windowed else seq_len
    nq = seq_len // q_rows
    sink_col = jnp.repeat(sinks.astype(F32), q_rows, axis=1).reshape(kvh, g * q_rows, 1)
    in_specs = [
        pl.BlockSpec((None, None, g, q_rows, hd), lambda i, h, j: (i, h, 0, j, 0)),
        pl.BlockSpec((None, None, seq_len, hd), lambda i, h, j: (i, h, 0, 0)),
        pl.BlockSpec((None, None, seq_len, hd), lambda i, h, j: (i, h, 0, 0)),
    ]
    args = [q, k, v]
    if windowed:
        s = k_ctx.shape[2]
        in_specs += [pl.BlockSpec((None, None, s, hd), lambda i, h, j: (i, h, 0, 0))] * 2
        args += [k_ctx, v_ctx]
    in_specs.append(pl.BlockSpec((None, g * q_rows, 1), lambda i, h, j: (h, 0, 0)))
    args.append(sink_col)
    return pl.pallas_call(
        functools.partial(_attn_kernel, seq_len=seq_len, q_rows=q_rows, windowed=windowed),
        out_shape=jax.ShapeDtypeStruct(q.shape, BF16),
        grid=(b, kvh, nq),
        in_specs=in_specs,
        out_specs=pl.BlockSpec((None, None, g, q_rows, hd), lambda i, h, j: (i, h, 0, j, 0)),
        compiler_params=_cparams(("arbitrary", "arbitrary", "arbitrary")),
        name="attention",
    )(*args)


def _router_kernel(x_ref, nw_ref, sh_ref, sc_ref, whi_ref, wlo_ref, b_ref, hrows_ref, idx_ref, gate_ref):
    x = x_ref[...]
    y = x * lax.rsqrt(jnp.mean(x * x, axis=-1, keepdims=True) + NORM_EPS) * nw_ref[...]
    hf = y * (1.0 + sc_ref[...]) + sh_ref[...]
    bm, d = hf.shape
    sub = d // LANES
    for s in range(sub):
        hrows_ref[pl.ds(s, bm, stride=sub), :] = hf[:, s * LANES:(s + 1) * LANES]
    h = hf.astype(BF16)
    logits = (jnp.dot(h, whi_ref[...], preferred_element_type=F32)
              + jnp.dot(h, wlo_ref[...], preferred_element_type=F32) + b_ref[...])
    lane = lax.broadcasted_iota(jnp.int32, logits.shape, 1)
    idx_out = jnp.zeros(logits.shape, jnp.int32)
    val_out = jnp.full(logits.shape, NEG_INF, F32)
    for kk in range(TOP_K):
        m = jnp.max(logits, axis=-1, keepdims=True)
        sel = jnp.min(jnp.where(logits == m, lane, LANES), axis=-1, keepdims=True)
        idx_out = jnp.where(lane == kk, sel, idx_out)
        val_out = jnp.where(lane == kk, m, val_out)
        logits = jnp.where(lane == sel, -jnp.inf, logits)
    e = jnp.exp(val_out - jnp.max(val_out, axis=-1, keepdims=True))
    idx_ref[...] = idx_out
    gate_ref[...] = e / jnp.sum(e, axis=-1, keepdims=True)


def norm_route(x, norm_w, layer, mods, lay, w_r, b_r):
    t, d = x.shape
    sub = d // LANES
    n_exp = w_r.shape[1]
    wp = jnp.zeros((d, LANES), F32).at[:, :n_exp].set(w_r)
    whi = wp.astype(BF16)
    wlo = (wp - whi.astype(F32)).astype(BF16)
    bp = jnp.full((1, LANES), -jnp.inf, F32).at[0, :n_exp].set(b_r)
    bm = 256
    cond = _cond_of_block(lay, bm)
    return pl.pallas_call(
        _router_kernel,
        out_shape=[jax.ShapeDtypeStruct((t * sub, LANES), F32), jax.ShapeDtypeStruct((t, LANES), jnp.int32),
                   jax.ShapeDtypeStruct((t, LANES), F32)],
        grid=(t // bm,),
        in_specs=[pl.BlockSpec((bm, d), lambda i: (i, 0)),
                  pl.BlockSpec((None, 1, d), lambda i: (layer, 0, 0)),
                  pl.BlockSpec((None, None, None, 1, d), lambda i: (layer, cond(i), 3, 0, 0)),
                  pl.BlockSpec((None, None, None, 1, d), lambda i: (layer, cond(i), 4, 0, 0)),
                  pl.BlockSpec((d, LANES), lambda i: (0, 0)),
                  pl.BlockSpec((d, LANES), lambda i: (0, 0)), pl.BlockSpec((1, LANES), lambda i: (0, 0))],
        out_specs=[pl.BlockSpec((bm * sub, LANES), lambda i: (i, 0))] + [pl.BlockSpec((bm, LANES), lambda i: (i, 0))] * 2,
        compiler_params=_cparams(("arbitrary",)),
        name="norm_route",
    )(x, norm_w, mods, mods, whi, wlo, bp)


def _gather_kernel(tok_ref, h_hbm, o_ref, buf, sem, *, sub, pitch):
    i = pl.program_id(0)
    n_steps = pl.num_programs(0)
    bm = o_ref.shape[0]

    def row_copy(step, r, slot):
        tok = tok_ref[step * bm + r]
        return pltpu.make_async_copy(h_hbm.at[pl.ds(pl.multiple_of(tok * sub, sub), sub), :],
                                     buf.at[slot, pl.ds(pl.multiple_of(r * pitch, GATHER_PAD), sub), :], sem.at[slot])

    def start_rows(step, slot):
        for r in range(bm):
            row_copy(step, r, slot).start()

    def finish_rows(step, slot):
        def body(r, c):
            row_copy(step, r, slot).wait()
            return c
        lax.fori_loop(0, bm, body, 0, unroll=8)
        for s in range(sub):
            o_ref[:, s * LANES:(s + 1) * LANES] = buf[slot, pl.ds(s, bm, stride=pitch), :].astype(o_ref.dtype)

    @pl.when(i == 0)
    def _():
        start_rows(0, 0)

    for slot in range(2):
        @pl.when(lax.rem(i, 2) == slot)
        def _(slot=slot):
            @pl.when(i + 1 < n_steps)
            def _():
                start_rows(i + 1, 1 - slot)
            finish_rows(i, slot)


def gather_rows(h_rows, d, slot_tok, bm):
    p = slot_tok.shape[0]
    sub = d // LANES
    pitch = sub + GATHER_PAD
    return pl.pallas_call(
        functools.partial(_gather_kernel, sub=sub, pitch=pitch),
        out_shape=jax.ShapeDtypeStruct((p, d), BF16),
        grid_spec=pltpu.PrefetchScalarGridSpec(
            num_scalar_prefetch=1,
            grid=(p // bm,),
            in_specs=[pl.BlockSpec(memory_space=pl.ANY)],
            out_specs=pl.BlockSpec((bm, d), lambda i, tok: (i, 0)),
            scratch_shapes=[pltpu.VMEM((2, bm * pitch, LANES), F32), pltpu.SemaphoreType.DMA((2,))],
        ),
        compiler_params=_cparams(("arbitrary",)),
        name="moe_gather",
    )(slot_tok, h_rows)


def _expert_rows(n, base, n_act, nb_total, x_hbm, o_hbm, xbuf, obuf, zbuf, xsem, osem, zsem, tn, compute):
    t, e = pl.program_id(0), pl.program_id(1)
    bm = xbuf.shape[1]
    look = MOE_RING - 1
    total = pl.num_programs(0) * n_act
    last_expert = e == pl.num_programs(1) - 1

    def x_copy(cnt):
        rows = pl.ds(pl.multiple_of(lax.rem(cnt, n_act) * bm, bm), bm)
        slot = lax.rem(cnt, MOE_RING)
        return pltpu.make_async_copy(x_hbm.at[rows, :], xbuf.at[slot], xsem.at[slot])

    def o_copy(cnt):
        rows = pl.ds(pl.multiple_of(lax.rem(cnt, n_act) * bm, bm), bm)
        cols = pl.ds(pl.multiple_of(lax.div(cnt, n_act) * tn, tn), tn)
        slot = lax.rem(cnt, 2)
        return pltpu.make_async_copy(obuf.at[slot], o_hbm.at[rows, cols], osem.at[slot])

    @pl.when(jnp.logical_and(t == 0, e == 0))
    def _():
        for c in range(look):
            @pl.when(c < total)
            def _(c=c):
                x_copy(jnp.int32(c)).start()

    cnt0 = t * n_act + base

    def visit(j, carry):
        cnt = cnt0 + j
        x_copy(cnt).wait()

        @pl.when(cnt + look < total)
        def _():
            x_copy(cnt + look).start()

        res = compute(xbuf[lax.rem(cnt, MOE_RING)])

        @pl.when(cnt >= 2)
        def _():
            o_copy(cnt - 2).wait()

        obuf[lax.rem(cnt, 2)] = res
        o_copy(cnt).start()
        return carry

    lax.fori_loop(0, n, visit, 0)

    @pl.when(jnp.logical_and(t == pl.num_programs(0) - 1, last_expert))
    def _():
        for back in (2, 1):
            @pl.when(total >= back)
            def _(back=back):
                o_copy(total - back).wait()

    @pl.when(jnp.logical_and(last_expert, n_act < nb_total))
    def _():
        zbuf[...] = jnp.zeros(zbuf.shape, zbuf.dtype)
        cols = pl.ds(pl.multiple_of(t * tn, tn), tn)

        def tail(blk, carry):
            cp = pltpu.make_async_copy(zbuf, o_hbm.at[pl.ds(pl.multiple_of(blk * bm, bm), bm), cols], zsem)
            cp.start()
            cp.wait()
            return carry

        lax.fori_loop(n_act, nb_total, tail, 0)


def _expert_up_kernel(nblk_ref, bstart_ref, nact_ref, x_hbm, wg_ref, wl_ref, bg_ref, bl_ref, o_hbm,
                      wg_bf, wl_bf, xbuf, obuf, zbuf, xsem, osem, zsem, *, nb_total):
    e = pl.program_id(1)
    n = nblk_ref[e]

    @pl.when(n > 0)
    def _():
        wg_bf[...] = wg_ref[...].astype(BF16)
        wl_bf[...] = wl_ref[...].astype(BF16)

    def compute(x):
        glu = jnp.minimum(jnp.dot(x, wg_bf[...], preferred_element_type=F32) + bg_ref[...], SWIGLU_LIMIT)
        lin = jnp.clip(jnp.dot(x, wl_bf[...], preferred_element_type=F32) + bl_ref[...], -SWIGLU_LIMIT, SWIGLU_LIMIT)
        return (glu * jax.nn.sigmoid(SWIGLU_ALPHA * glu) * (lin + 1.0)).astype(obuf.dtype)

    _expert_rows(n, bstart_ref[e], nact_ref[0], nb_total, x_hbm, o_hbm, xbuf, obuf, zbuf, xsem, osem, zsem,
                 wg_bf.shape[1], compute)


def _expert_down_kernel(nblk_ref, bstart_ref, nact_ref, x_hbm, w_ref, b_ref, o_hbm,
                        w_bf, xbuf, obuf, zbuf, xsem, osem, zsem, *, nb_total):
    e = pl.program_id(1)
    n = nblk_ref[e]

    @pl.when(n > 0)
    def _():
        w_bf[...] = w_ref[...].astype(BF16)

    def compute(x):
        return jnp.dot(x, w_bf[...], preferred_element_type=F32) + b_ref[...]

    _expert_rows(n, bstart_ref[e], nact_ref[0], nb_total, x_hbm, o_hbm, xbuf, obuf, zbuf, xsem, osem, zsem,
                 w_bf.shape[1], compute)


def expert_ffn(xg, sched, layer, w_gu, b_gu, w_dn, b_dn):
    p, d = xg.shape
    nb = p // MOE_BM
    ff = w_dn.shape[2]
    tn, tn2 = MOE_UP_TILE, MOE_DOWN_TILE
    nt = ff // tn
    n_exp = w_gu.shape[1]
    bgu = b_gu.reshape(b_gu.shape[0], n_exp, 1, 2 * ff)
    dma = pltpu.SemaphoreType.DMA
    hidden = pl.pallas_call(
        functools.partial(_expert_up_kernel, nb_total=nb),
        out_shape=jax.ShapeDtypeStruct((p, ff), BF16),
        grid_spec=pltpu.PrefetchScalarGridSpec(
            num_scalar_prefetch=3,
            grid=(nt, n_exp),
            in_specs=[
                pl.BlockSpec(memory_space=pl.ANY),
                pl.BlockSpec((None, None, d, tn), lambda t, e, *_: (layer, e, 0, t)),
                pl.BlockSpec((None, None, d, tn), lambda t, e, *_: (layer, e, 0, nt + t)),
                pl.BlockSpec((None, None, 1, tn), lambda t, e, *_: (layer, e, 0, t)),
                pl.BlockSpec((None, None, 1, tn), lambda t, e, *_: (layer, e, 0, nt + t)),
            ],
            out_specs=pl.BlockSpec(memory_space=pl.ANY),
            scratch_shapes=[pltpu.VMEM((d, tn), BF16), pltpu.VMEM((d, tn), BF16),
                            pltpu.VMEM((MOE_RING, MOE_BM, d), BF16), pltpu.VMEM((2, MOE_BM, tn), BF16),
                            pltpu.VMEM((MOE_BM, tn), BF16), dma((MOE_RING,)), dma((2,)), dma(())],
        ),
        compiler_params=_cparams(("arbitrary", "arbitrary")),
        name="expert_up",
    )(*sched, xg, w_gu, w_gu, bgu, bgu)
    return pl.pallas_call(
        functools.partial(_expert_down_kernel, nb_total=nb),
        out_shape=jax.ShapeDtypeStruct((p, d), F32),
        grid_spec=pltpu.PrefetchScalarGridSpec(
            num_scalar_prefetch=3,
            grid=(d // tn2, n_exp),
            in_specs=[
                pl.BlockSpec(memory_space=pl.ANY),
                pl.BlockSpec((None, None, ff, tn2), lambda t, e, *_: (layer, e, 0, t)),
                pl.BlockSpec((None, None, 1, tn2), lambda t, e, *_: (layer, e, 0, t)),
            ],
            out_specs=pl.BlockSpec(memory_space=pl.ANY),
            scratch_shapes=[pltpu.VMEM((ff, tn2), BF16),
                            pltpu.VMEM((MOE_RING, MOE_BM, ff), BF16), pltpu.VMEM((2, MOE_BM, tn2), F32),
                            pltpu.VMEM((MOE_BM, tn2), F32), dma((MOE_RING,)), dma((2,)), dma(())],
        ),
        compiler_params=_cparams(("arbitrary", "arbitrary")),
        name="expert_down",
    )(*sched, hidden, w_dn, b_dn.reshape(b_dn.shape[0], n_exp, 1, d))


def _combine_kernel(*refs, modulated):
    it = iter(refs)
    dest_ref, yb_hbm, x_ref, gate_ref, mod_ref, nw_ref = (next(it) for _ in range(6))
    sh_ref, sc_ref = (next(it), next(it)) if modulated else (None, None)
    o_ref, h_ref, buf, sem = next(it), next(it), next(it), next(it)
    bt = x_ref.shape[0]

    def row_copy(i):
        return pltpu.make_async_copy(yb_hbm.at[pl.ds(dest_ref[0, 0, i], 1), :], buf.at[pl.ds(i, 1), :], sem)

    def wait(i, c):
        row_copy(i).wait()
        return c

    for i in range(TOP_K * bt):
        row_copy(i).start()
    lax.fori_loop(0, TOP_K * bt, wait, 0, unroll=8)
    g = gate_ref[...]
    y = g[:, 0:1] * buf[0:bt, :]
    for kk in range(1, TOP_K):
        y = y + g[:, kk:kk + 1] * buf[kk * bt:(kk + 1) * bt, :]
    x_new = x_ref[...] + mod_ref[...] * y
    o_ref[...] = x_new
    nrm = x_new * lax.rsqrt(jnp.mean(x_new * x_new, axis=-1, keepdims=True) + NORM_EPS) * nw_ref[...]
    if modulated:
        nrm = nrm * (1.0 + sc_ref[...]) + sh_ref[...]
    h_ref[...] = nrm.astype(h_ref.dtype)


def moe_combine(x, yb, dest, gates, mods, layer, lay, next_norm_w, next_layer):
    t, d = x.shape
    bt = 128
    cond = _cond_of_block(lay, bt)
    modulated = next_layer is not None
    dest_blk = dest.reshape(t // bt, bt, TOP_K).transpose(0, 2, 1).reshape(t // bt, 1, TOP_K * bt)
    in_specs = [pl.BlockSpec((1, 1, TOP_K * bt), lambda i: (i, 0, 0), memory_space=pltpu.SMEM),
                pl.BlockSpec(memory_space=pl.ANY),
                pl.BlockSpec((bt, d), lambda i: (i, 0)),
                pl.BlockSpec((bt, LANES), lambda i: (i, 0)),
                pl.BlockSpec((None, None, None, 1, d), lambda i: (layer, cond(i), 5, 0, 0))]
    args = [dest_blk, yb, x, gates, mods]
    if modulated:
        in_specs += [pl.BlockSpec((None, 1, d), lambda i: (next_layer, 0, 0)),
                     pl.BlockSpec((None, None, None, 1, d), lambda i: (next_layer, cond(i), 0, 0, 0)),
                     pl.BlockSpec((None, None, None, 1, d), lambda i: (next_layer, cond(i), 1, 0, 0))]
        args += [next_norm_w, mods, mods]
    else:
        in_specs.append(pl.BlockSpec((1, d), lambda i: (0, 0)))
        args.append(next_norm_w.reshape(1, d))
    return pl.pallas_call(
        functools.partial(_combine_kernel, modulated=modulated),
        out_shape=[jax.ShapeDtypeStruct((t, d), F32), jax.ShapeDtypeStruct((t, d), BF16 if modulated else F32)],
        grid=(t // bt,),
        in_specs=in_specs,
        out_specs=[pl.BlockSpec((bt, d), lambda i: (i, 0))] * 2,
        scratch_shapes=[pltpu.VMEM((TOP_K * bt, d), F32), pltpu.SemaphoreType.DMA(())],
        compiler_params=_cparams(("arbitrary",)),
        name="moe_combine",
    )(*args)


def _moe_schedule(top_i, n_exp):
    t = top_i.shape[0]
    n_asg = t * TOP_K
    flat_e = top_i.reshape(n_asg)
    onehot = (flat_e[:, None] == jnp.arange(n_exp, dtype=jnp.int32)[None, :]).astype(jnp.int32)
    oh = onehot.astype(F32).reshape(n_asg // LANES, LANES, n_exp)
    within = jnp.einsum("ij,bjk->bik", jnp.tril(jnp.ones((LANES, LANES), F32)), oh)
    totals = within[:, -1, :]
    csum = (within + (jnp.cumsum(totals, axis=0) - totals)[:, None, :]).reshape(n_asg, n_exp).astype(jnp.int32)
    counts = csum[-1]
    pcounts = (counts + MOE_BM - 1) // MOE_BM * MOE_BM
    pends = jnp.cumsum(pcounts)
    pstarts = pends - pcounts
    dest = jnp.sum(onehot * (csum - 1 + pstarts[None, :]), axis=1)
    nb = -(-(n_asg + n_exp * (MOE_BM - 1)) // MOE_BM)
    slot_tok = jnp.zeros((nb * MOE_BM,), jnp.int32).at[dest].set(jnp.arange(n_asg, dtype=jnp.int32) // TOP_K)
    nblk = (pcounts // MOE_BM).astype(jnp.int32)
    bstart = (pstarts // MOE_BM).astype(jnp.int32)
    n_act = (pends[-1:] // MOE_BM).astype(jnp.int32)
    return dest.reshape(t, TOP_K), slot_tok, (nblk, bstart, n_act)


def moe_layer(x, norm_w, layer, mods, lay, w_r, b_r, w_gu, b_gu, w_dn, b_dn, next_norm_w, next_layer):
    h_rows, top_i, gates = norm_route(x, norm_w, layer, mods, lay, w_r[layer], b_r[layer])
    dest, slot_tok, sched = _moe_schedule(top_i[:, :TOP_K], w_gu.shape[1])
    xg = gather_rows(h_rows, x.shape[1], slot_tok, MOE_BM)
    yb = expert_ffn(xg, sched, layer, w_gu, b_gu, w_dn, b_dn)
    return moe_combine(x, yb, dest, gates, mods, layer, lay, next_norm_w, next_layer)


def gla_layer(x, h, layer, idx, mods, lay, state_gla, w_in, w_gk1, w_gk2, b_gk, norm_w, w_out):
    d = h.shape[1]
    y = matmul(h, w_in, idx)
    w1 = jnp.zeros((1, d, 2 * RANK_PAD), F32)
    w1 = w1.at[0, :, :GLA_RANK].set(w_gk1[idx, 0]).at[0, :, RANK_PAD:RANK_PAD + GLA_RANK].set(w_gk1[idx, 1])
    r = matmul(h, w1, 0)
    w2 = jnp.zeros((2, RANK_PAD, GLA_QK), F32).at[:, :GLA_RANK, :].set(w_gk2[idx])
    gla_args = (r, w2, b_gk[idx].reshape(2, 1, GLA_QK), norm_w[idx].reshape(1, DV_GLA))
    common = dict(n_heads=H_GLA, dk=DK_GLA, dv=DV_GLA, gla=True, gla_args=gla_args)
    o_p, st = linear_scan(y, seq_len=lay.l_p, n_seq=lay.n_p, row_block0=0, want_state=True, **common)
    o_s, _ = linear_scan(y, seq_len=lay.l_s, n_seq=lay.n_s, row_block0=lay.t_p // lay.l_s,
                         s0=state_gla, s0_idx=idx, **common)
    o = jnp.concatenate([o_p, o_s], axis=0)
    return matmul(o, w_out, idx, resid=(x, mods, layer, 2), lay=lay), st


def ret_layer(x, h, layer, idx, mods, lay, state_ret, w_in, decay_logit, w_out):
    y = matmul(h, w_in, idx)
    common = dict(n_heads=H_RET, dk=DK_RET, dv=DV_RET, gla=False, decay_logit=decay_logit[idx])
    o_p, st = linear_scan(y, seq_len=lay.l_p, n_seq=lay.n_p, row_block0=0, want_state=True, **common)
    o_s, _ = linear_scan(y, seq_len=lay.l_s, n_seq=lay.n_s, row_block0=lay.t_p // lay.l_s,
                         rope_tabs=_rope_tables(lay.l_s, DK_RET), s0=state_ret, s0_idx=idx, **common)
    o = jnp.concatenate([o_p, o_s], axis=0)
    return matmul(o, w_out, idx, resid=(x, mods, layer, 2), lay=lay), st


def swa_layer(x, h, layer, idx, mods, lay, cache_k, cache_v, w_qkv, b_qkv, sinks, w_o, b_o):
    qkv = matmul(h, w_qkv, idx, bias=b_qkv)
    sk = sinks[idx].reshape(KV_HEADS, GROUP)

    def heads(a, n_seq, seq_len, nh):
        return a.reshape(n_seq, seq_len, nh, HD_SWA).transpose(0, 2, 1, 3)

    qkv_p = qkv[:lay.t_p]
    k_p, v_p = qkv_p[:, SWA_Q:SWA_Q + SWA_KV], qkv_p[:, SWA_Q + SWA_KV:]
    q_h = heads(qkv_p[:, :SWA_Q], lay.n_p, lay.l_p, N_HEADS_SWA).reshape(lay.n_p, KV_HEADS, GROUP, lay.l_p, HD_SWA)
    o_p = attention(q_h, heads(k_p, lay.n_p, lay.l_p, KV_HEADS), heads(v_p, lay.n_p, lay.l_p, KV_HEADS), sk)
    o_p = o_p.reshape(lay.n_p, N_HEADS_SWA, lay.l_p, HD_SWA).transpose(0, 2, 1, 3).reshape(lay.t_p, SWA_Q)

    qkv_s = qkv[lay.t_p:]
    qk_rot = rope64(qkv_s[:, :SWA_Q + SWA_KV], lay.l_s)
    q_h = heads(qk_rot[:, :SWA_Q], lay.n_s, lay.l_s, N_HEADS_SWA).reshape(lay.n_s, KV_HEADS, GROUP, lay.l_s, HD_SWA)
    k_h = heads(qk_rot[:, SWA_Q:], lay.n_s, lay.l_s, KV_HEADS)
    v_h = heads(qkv_s[:, SWA_Q + SWA_KV:], lay.n_s, lay.l_s, KV_HEADS)
    kc = cache_k[:, idx].transpose(0, 2, 1, 3)
    vc = cache_v[:, idx].transpose(0, 2, 1, 3)
    o_s = attention(q_h, k_h, v_h, sk, k_ctx=kc, v_ctx=vc)
    o_s = o_s.reshape(lay.n_s, N_HEADS_SWA, lay.l_s, HD_SWA).transpose(0, 2, 1, 3).reshape(lay.n_s * lay.l_s, SWA_Q)

    o = jnp.concatenate([o_p, o_s], axis=0)
    x_new = matmul(o, w_o, idx, bias=b_o, resid=(x, mods, layer, 2), lay=lay)
    new_k = k_p.reshape(lay.n_p, lay.l_p, KV_HEADS, HD_SWA)
    new_v = v_p.reshape(lay.n_p, lay.l_p, KV_HEADS, HD_SWA)
    return x_new, new_k, new_v


def kernel(x_prompt, x_sample, state_gla, state_ret, cache_k, cache_v, c, c_ctx, norm_mix_w, norm_ffn_w, w_ada, b_ada, gla_w_in, gla_w_gk1, gla_w_gk2, gla_b_gk, gla_norm_w, gla_w_out, ret_w_in, ret_decay_logit, ret_w_out, swa_w_qkv, swa_b_qkv, swa_sinks, swa_w_o, swa_b_o, moe_w_router, moe_b_router, moe_w_gate_up, moe_b_gate_up, moe_w_down, moe_b_down, final_norm_w):
    n_p, l_p, d = x_prompt.shape
    n_s, l_s, _ = x_sample.shape
    lay = Layout(n_p, l_p, n_s, l_s)
    depth = w_ada.shape[0]
    x = jnp.concatenate([x_prompt.reshape(n_p * l_p, d), x_sample.reshape(n_s * l_s, d)], axis=0)
    cond = jnp.zeros((COND_ROWS, d), F32).at[0].set(c_ctx).at[1:1 + n_s].set(c)
    mods = adaln_all(cond, w_ada, b_ada).reshape(depth, COND_ROWS, N_MOD, 1, d)
    nmw = norm_mix_w.reshape(depth, 1, d)
    nfw = norm_ffn_w.reshape(depth, 1, d)
    new_gla, new_ret, new_k, new_v = [], [], [], []
    h = norm_modulate(x, nmw, 0, mods, 0, lay, BF16)
    for layer in range(depth):
        kind, idx = layer % N_MIXERS, layer // N_MIXERS
        if kind == 0:
            x, st = gla_layer(x, h, layer, idx, mods, lay, state_gla, gla_w_in, gla_w_gk1, gla_w_gk2, gla_b_gk,
                              gla_norm_w, gla_w_out)
            new_gla.append(st)
        elif kind == 1:
            x, st = ret_layer(x, h, layer, idx, mods, lay, state_ret, ret_w_in, ret_decay_logit, ret_w_out)
            new_ret.append(st)
        else:
            x, kc, vc = swa_layer(x, h, layer, idx, mods, lay, cache_k, cache_v, swa_w_qkv, swa_b_qkv, swa_sinks,
                                  swa_w_o, swa_b_o)
            new_k.append(kc)
            new_v.append(vc)
        last = layer == depth - 1
        x, h = moe_layer(x, nfw, layer, mods, lay, moe_w_router, moe_b_router, moe_w_gate_up, moe_b_gate_up,
                         moe_w_down, moe_b_down, final_norm_w if last else nmw, None if last else layer + 1)
    y = h
    y_prompt = y[:lay.t_p].reshape(n_p, l_p, d)
    y_sample = y[lay.t_p:].reshape(n_s, l_s, d)
    return (y_prompt, y_sample, jnp.stack(new_gla, axis=1), jnp.stack(new_ret, axis=1),
            jnp.stack(new_k, axis=1), jnp.stack(new_v, axis=1))
```
